```python
import jax
import jax.numpy as jnp
from jax import lax
import numpy as np

D_MODEL = 1024
BATCH = 8
SEQ = 2048
DEPTH = 4
DEC_BATCH = 128
DEC_SEQ = 4
PAST_LEN = 8192
PAGE_SIZE = 128

N_MIXERS = 3
HEAD_DIM = 64
N_HEADS = D_MODEL // HEAD_DIM
N_KV_HEADS = 4
GROUP = N_HEADS // N_KV_HEADS
ATT_WIDTH = N_HEADS * HEAD_DIM
KV_WIDTH = N_KV_HEADS * HEAD_DIM
ATT_SCALE = HEAD_DIM ** -0.5
Q_BLOCK = 128
WINDOW = 128
ROPE_THETA = 500000.0
ROT_DIM = HEAD_DIM // 4
CONV_WIDTH = 31
CONV_DIM = D_MODEL
PE_DIM = 256
NORM_EPS = 1e-6
FORGET_BIAS_MIN = 2.0
FORGET_BIAS_MAX = 10.0
N_FOX = (DEPTH + N_MIXERS - 1) // N_MIXERS
N_SWA = (DEPTH + N_MIXERS - 2) // N_MIXERS
N_CONV = DEPTH // N_MIXERS
FOX_IN = 2 * ATT_WIDTH + 2 * KV_WIDTH + N_HEADS
SWA_IN = 2 * ATT_WIDTH + 2 * KV_WIDTH
CONV_IN = 3 * CONV_DIM

kernel_name = 'hybrid_fox_swa_conformer_decode_step'


def rms_norm(x, g):
    xf = x.astype(jnp.float32)
    y = xf * lax.rsqrt(jnp.mean(xf * xf, axis=-1, keepdims=True) + NORM_EPS)
    return (y * g.astype(jnp.float32)).astype(x.dtype)


def layer_norm(x, g, b):
    xf = x.astype(jnp.float32)
    xc = xf - jnp.mean(xf, axis=-1, keepdims=True)
    y = xc * lax.rsqrt(jnp.mean(xc * xc, axis=-1, keepdims=True) + NORM_EPS)
    return (y * g.astype(jnp.float32) + b.astype(jnp.float32)).astype(x.dtype)


def forget_bias_spread():
    return jnp.linspace(FORGET_BIAS_MIN, FORGET_BIAS_MAX, N_HEADS, dtype=jnp.float32)


def partial_rope(x, pos):
    half = ROT_DIM // 2
    inv_freq = ROPE_THETA ** (-jnp.arange(half, dtype=jnp.float32) / half)
    ang = pos.astype(jnp.float32)[:, None] * inv_freq[None, :]
    cos = jnp.cos(ang)[:, None, :]
    sin = jnp.sin(ang)[:, None, :]
    xr = x[..., :ROT_DIM].astype(jnp.float32)
    x1, x2 = xr[..., :half], xr[..., half:]
    rot = jnp.concatenate([x1 * cos - x2 * sin, x2 * cos + x1 * sin], axis=-1).astype(x.dtype)
    return jnp.concatenate([rot, x[..., ROT_DIM:]], axis=-1)


def split_q(q):
    return q.reshape(q.shape[:2] + (N_KV_HEADS, GROUP, HEAD_DIM))


def split_kv(k):
    return k.reshape(k.shape[:2] + (N_KV_HEADS, HEAD_DIM))


def merge_heads(o):
    return o.reshape(o.shape[:2] + (ATT_WIDTH,))


def heads_last(a):
    return jnp.transpose(a.reshape(a.shape[:2] + (N_KV_HEADS, GROUP)), (0, 2, 3, 1))


def sink_softmax(s, sinks):
    sink = jnp.broadcast_to(sinks.astype(jnp.float32).reshape(N_KV_HEADS, GROUP, 1, 1), s.shape[:-1] + (1,))
    p = jax.nn.softmax(jnp.concatenate([s, sink], axis=-1), axis=-1)
    return p[..., :-1]


def fox_project(h, w_in, b_f):
    u = h @ w_in
    q, k, v, fl, z = jnp.split(u, [ATT_WIDTH, ATT_WIDTH + KV_WIDTH, ATT_WIDTH + 2 * KV_WIDTH,
                                   ATT_WIDTH + 2 * KV_WIDTH + N_HEADS], axis=-1)
    lf = jax.nn.log_sigmoid(fl.astype(jnp.float32) + b_f.astype(jnp.float32))
    return split_q(q), split_kv(k), split_kv(v), lf, z


def fox_attend_prompt(q, k, v, lf):
    L = q.shape[1]
    c = heads_last(jnp.cumsum(lf, axis=1))
    outs = []
    for blk in range(L // Q_BLOCK):
        q0, q1 = blk * Q_BLOCK, (blk + 1) * Q_BLOCK
        s = jnp.einsum('bqkgd,bskd->bkgqs', q[:, q0:q1], k[:, :q1]).astype(jnp.float32) * ATT_SCALE
        s = s + c[..., q0:q1, None] - c[..., None, :q1]
        causal = jnp.arange(q0, q1)[:, None] >= jnp.arange(q1)[None, :]
        p = jax.nn.softmax(jnp.where(causal, s, -jnp.inf), axis=-1).astype(v.dtype)
        outs.append(jnp.einsum('bkgqs,bskd->bqkgd', p, v[:, :q1]))
    return merge_heads(jnp.concatenate(outs, axis=1))


def fox_attend_sample(q, k_new, v_new, lf_new, k_past, v_past, lf_past):
    T = q.shape[1]
    P = k_past.shape[1]
    suffix = heads_last(lax.cumsum(lf_past, axis=1, reverse=True) - lf_past)
    c_new = heads_last(jnp.cumsum(lf_new, axis=1))
    s_past = jnp.einsum('btkgd,bskd->bkgts', q, k_past).astype(jnp.float32) * ATT_SCALE
    s_past = s_past + c_new[..., :, None] + suffix[..., None, :]
    s_new = jnp.einsum('btkgd,bukd->bkgtu', q, k_new).astype(jnp.float32) * ATT_SCALE
    s_new = s_new + c_new[..., :, None] - c_new[..., None, :]
    causal = jnp.arange(T)[:, None] >= jnp.arange(T)[None, :]
    s_new = jnp.where(causal, s_new, -jnp.inf)
    p = jax.nn.softmax(jnp.concatenate([s_past, s_new], axis=-1), axis=-1).astype(v_new.dtype)
    o = jnp.einsum('bkgts,bskd->btkgd', p[..., :P], v_past) + jnp.einsum('bkgtu,bukd->btkgd', p[..., P:], v_new)
    return merge_heads(o)


def swa_project(h, w_in, pos):
    u = h @ w_in
    q, k, v, z = jnp.split(u, [ATT_WIDTH, ATT_WIDTH + KV_WIDTH, ATT_WIDTH + 2 * KV_WIDTH], axis=-1)
    q = partial_rope(q.reshape(q.shape[:2] + (N_HEADS, HEAD_DIM)), pos)
    k = partial_rope(split_kv(k), pos)
    return q.reshape(q.shape[:2] + (N_KV_HEADS, GROUP, HEAD_DIM)), k, split_kv(v), z


def swa_attend_prompt(q, k, v, sinks):
    B, L = q.shape[:2]
    nb = L // Q_BLOCK
    qb = q.reshape(B, nb, Q_BLOCK, N_KV_HEADS, GROUP, HEAD_DIM)

    def band(a):
        ab = a.reshape(B, nb, Q_BLOCK, N_KV_HEADS, HEAD_DIM)
        prev = jnp.pad(ab[:, :-1], ((0, 0), (1, 0), (0, 0), (0, 0), (0, 0)))
        return jnp.concatenate([prev, ab], axis=2)

    kb, vb = band(k), band(v)
    s = jnp.einsum('bnqkgd,bnskd->bnkgqs', qb, kb).astype(jnp.float32) * ATT_SCALE
    qi = jnp.arange(Q_BLOCK)[:, None]
    si = jnp.arange(2 * Q_BLOCK)[None, :]
    rel = Q_BLOCK + qi - si
    first = (jnp.arange(nb) == 0)[:, None, None]
    allowed = (rel >= 0) & (rel < WINDOW) & ~(first & (si < Q_BLOCK))
    s = jnp.where(allowed[None, :, None, None], s, -jnp.inf)
    p = sink_softmax(s, sinks).astype(v.dtype)
    o = jnp.einsum('bnkgqs,bnskd->bnqkgd', p, vb)
    return o.reshape(B, L, ATT_WIDTH)


def swa_attend_sample(q, k_new, v_new, buf_k, buf_v, sinks, past_len):
    T = q.shape[1]
    buf = buf_k.shape[1]
    kk = jnp.concatenate([buf_k, k_new], axis=1)
    vv = jnp.concatenate([buf_v, v_new], axis=1)
    kpos = past_len - buf + jnp.arange(buf + T)
    qpos = past_len + jnp.arange(T)
    rel = qpos[:, None] - kpos[None, :]
    allowed = (rel >= 0) & (rel < WINDOW)
    s = jnp.einsum('btkgd,bskd->bkgts', q, kk).astype(jnp.float32) * ATT_SCALE
    s = jnp.where(allowed, s, -jnp.inf)
    p = sink_softmax(s, sinks).astype(v_new.dtype)
    o = jnp.einsum('bkgts,bskd->btkgd', p, vv)
    return merge_heads(o), kk[:, -buf:], vv[:, -buf:]


def conv_project(h, w_in):
    a, g, z = jnp.split(h @ w_in, 3, axis=-1)
    return a * jax.nn.sigmoid(g), z


def conv_branch(glu_padded, z, dw_w, dw_b, ln_g, ln_b, w_out):
    y = lax.conv_general_dilated(glu_padded, dw_w[:, None, :].astype(glu_padded.dtype), window_strides=(1,),
                                 padding='VALID', dimension_numbers=('NWC', 'WIO', 'NWC'),
                                 feature_group_count=CONV_DIM) + dw_b
    y = jax.nn.silu(layer_norm(y, ln_g, ln_b))
    return (y * jax.nn.silu(z)) @ w_out


def per_layer_embed(x, p, w_proj, norm_g, w_gate, b_gate):
    gate = jax.nn.sigmoid(rms_norm(x, norm_g) @ w_gate + b_gate)
    return x + gate * (p @ w_proj)


def setup_inputs(seed: int = 0) -> dict:
    key = jax.random.key(seed)
    ks = iter(jax.random.split(key, 40))

    def nrm(shape, scale=1.0):
        return jax.random.normal(next(ks), shape, jnp.float32) * scale

    n_pages = PAST_LEN // PAGE_SIZE
    n_pool = (DEC_BATCH * n_pages * 5) // 4
    window_buf = min(WINDOW, PAST_LEN)
    page_table = jax.random.permutation(next(ks), n_pool)[: DEC_BATCH * n_pages]
    page_table = page_table.reshape(DEC_BATCH, n_pages).astype(jnp.int32)
    fb = forget_bias_spread()
    return {
        'x_prompt': nrm((BATCH, SEQ, D_MODEL)),
        'x_sample': nrm((DEC_BATCH, DEC_SEQ, D_MODEL)),
        'cache_fox_k': nrm((N_FOX, n_pool, PAGE_SIZE, N_KV_HEADS, HEAD_DIM)),
        'cache_fox_v': nrm((N_FOX, n_pool, PAGE_SIZE, N_KV_HEADS, HEAD_DIM)),
        'cache_fox_lf': jax.nn.log_sigmoid(fb + nrm((N_FOX, n_pool, PAGE_SIZE, N_HEADS))),
        'page_table': page_table,
        'state_swa_k': nrm((N_SWA, DEC_BATCH, window_buf, N_KV_HEADS, HEAD_DIM)),
        'state_swa_v': nrm((N_SWA, DEC_BATCH, window_buf, N_KV_HEADS, HEAD_DIM)),
        'state_conv': nrm((N_CONV, DEC_BATCH, CONV_WIDTH - 1, CONV_DIM), 0.5),
        'p_prompt': nrm((DEPTH, BATCH, SEQ, PE_DIM)),
        'p_sample': nrm((DEPTH, DEC_BATCH, DEC_SEQ, PE_DIM)),
        'norm_g': 1.0 + nrm((DEPTH, D_MODEL), 0.05),
        'fox_w_in': nrm((N_FOX, D_MODEL, FOX_IN), D_MODEL ** -0.5),
        'fox_b_f': fb + nrm((N_FOX, N_HEADS), 0.1),
        'fox_w_out': nrm((N_FOX, ATT_WIDTH, D_MODEL), ATT_WIDTH ** -0.5),
        'swa_w_in': nrm((N_SWA, D_MODEL, SWA_IN), D_MODEL ** -0.5),
        'swa_sinks': nrm((N_SWA, N_HEADS), 0.5),
        'swa_w_out': nrm((N_SWA, ATT_WIDTH, D_MODEL), ATT_WIDTH ** -0.5),
        'conv_w_in': nrm((N_CONV, D_MODEL, CONV_IN), D_MODEL ** -0.5),
        'conv_dw_w': nrm((N_CONV, CONV_WIDTH, CONV_DIM), CONV_WIDTH ** -0.5),
        'conv_dw_b': nrm((N_CONV, CONV_DIM), 0.02),
        'conv_ln_g': 1.0 + nrm((N_CONV, CONV_DIM), 0.05),
        'conv_ln_b': nrm((N_CONV, CONV_DIM), 0.02),
        'conv_w_out': nrm((N_CONV, CONV_DIM, D_MODEL), CONV_DIM ** -0.5),
        'pe_w_proj': nrm((DEPTH, PE_DIM, D_MODEL), PE_DIM ** -0.5),
        'pe_norm_g': 1.0 + nrm((DEPTH, D_MODEL), 0.05),
        'pe_w_gate': nrm((DEPTH, D_MODEL, D_MODEL), D_MODEL ** -0.5),
        'pe_b_gate': nrm((DEPTH, D_MODEL), 0.02),
        'final_norm_g': 1.0 + nrm((D_MODEL,), 0.05),
    }


def reference(x_prompt, x_sample, cache_fox_k, cache_fox_v, cache_fox_lf, page_table,
              state_swa_k, state_swa_v, state_conv, p_prompt, p_sample,
              norm_g, fox_w_in, fox_b_f, fox_w_out, swa_w_in, swa_sinks, swa_w_out,
              conv_w_in, conv_dw_w, conv_dw_b, conv_ln_g, conv_ln_b, conv_w_out,
              pe_w_proj, pe_norm_g, pe_w_gate, pe_b_gate, final_norm_g):
    n_dec = x_sample.shape[0]
    past_len = page_table.shape[1] * PAGE_SIZE
    pos_prompt = jnp.arange(x_prompt.shape[1], dtype=jnp.int32)
    pos_sample = past_len + jnp.arange(x_sample.shape[1], dtype=jnp.int32)
    buf = state_swa_k.shape[2]

    fox_k_p, fox_v_p, fox_lf_p, fox_k_s, fox_v_s, fox_lf_s = [], [], [], [], [], []
    swa_k_p, swa_v_p, swa_k_s, swa_v_s = [], [], [], []
    conv_p, conv_s = [], []

    xp, xs = x_prompt, x_sample
    for i in range(DEPTH):
        kind, j = i % N_MIXERS, i // N_MIXERS
        hp, hs = rms_norm(xp, norm_g[i]), rms_norm(xs, norm_g[i])
        if kind == 0:
            q, k, v, lf, z = fox_project(hp, fox_w_in[j], fox_b_f[j])
            yp = (fox_attend_prompt(q, k, v, lf) * jax.nn.silu(z)) @ fox_w_out[j]
            fox_k_p.append(k)
            fox_v_p.append(v)
            fox_lf_p.append(lf)
            q, k, v, lf, z = fox_project(hs, fox_w_in[j], fox_b_f[j])
            k_past = cache_fox_k[j][page_table].reshape(n_dec, past_len, N_KV_HEADS, HEAD_DIM)
            v_past = cache_fox_v[j][page_table].reshape(n_dec, past_len, N_KV_HEADS, HEAD_DIM)
            lf_past = cache_fox_lf[j][page_table].reshape(n_dec, past_len, N_HEADS).astype(jnp.float32)
            ys = (fox_attend_sample(q, k, v, lf, k_past, v_past, lf_past) * jax.nn.silu(z)) @ fox_w_out[j]
            fox_k_s.append(k)
            fox_v_s.append(v)
            fox_lf_s.append(lf)
        elif kind == 1:
            q, k, v, z = swa_project(hp, swa_w_in[j], pos_prompt)
            yp = (swa_attend_prompt(q, k, v, swa_sinks[j]) * jax.nn.silu(z)) @ swa_w_out[j]
            swa_k_p.append(k[:, -buf:])
            swa_v_p.append(v[:, -buf:])
            q, k, v, z = swa_project(hs, swa_w_in[j], pos_sample)
            o, nk, nv = swa_attend_sample(q, k, v, state_swa_k[j], state_swa_v[j], swa_sinks[j], past_len)
            ys = (o * jax.nn.silu(z)) @ swa_w_out[j]
            swa_k_s.append(nk)
            swa_v_s.append(nv)
        else:
            glu, z = conv_project(hp, conv_w_in[j])
            padded = jnp.pad(glu, ((0, 0), (CONV_WIDTH - 1, 0), (0, 0)))
            yp = conv_branch(padded, z, conv_dw_w[j], conv_dw_b[j], conv_ln_g[j], conv_ln_b[j], conv_w_out[j])
            conv_p.append(glu[:, -(CONV_WIDTH - 1):])
            glu, z = conv_project(hs, conv_w_in[j])
            padded = jnp.concatenate([state_conv[j].astype(glu.dtype), glu], axis=1)
            ys = conv_branch(padded, z, conv_dw_w[j], conv_dw_b[j], conv_ln_g[j], conv_ln_b[j], conv_w_out[j])
            conv_s.append(padded[:, -(CONV_WIDTH - 1):])
        xp = per_layer_embed(xp + yp, p_prompt[i], pe_w_proj[i], pe_norm_g[i], pe_w_gate[i], pe_b_gate[i])
        xs = per_layer_embed(xs + ys, p_sample[i], pe_w_proj[i], pe_norm_g[i], pe_w_gate[i], pe_b_gate[i])

    y_prompt = rms_norm(xp, final_norm_g)
    y_sample = rms_norm(xs, final_norm_g)
    return (y_prompt, y_sample,
            jnp.stack(fox_k_p), jnp.stack(fox_v_p), jnp.stack(fox_lf_p),
            jnp.stack(fox_k_s), jnp.stack(fox_v_s), jnp.stack(fox_lf_s),
            jnp.stack(swa_k_p), jnp.stack(swa_v_p), jnp.stack(swa_k_s), jnp.stack(swa_v_s),
            jnp.stack(conv_p), jnp.stack(conv_s))
```

```python
import functools

import jax
import jax.numpy as jnp
from jax import lax
from jax.experimental import pallas as pl
from jax.experimental.pallas import tpu as pltpu

F32 = jnp.float32
BF16 = jnp.bfloat16

D_MODEL = 1024
HEAD_DIM = 64
N_HEADS = 16
N_KV = 4
GROUP = 4
KV_WIDTH = N_KV * HEAD_DIM
ATT_SCALE = HEAD_DIM ** -0.5
N_MIXERS = 3
Q_BLOCK = 128
WINDOW = 128
ROPE_THETA = 500000.0
ROT_DIM = HEAD_DIM // 4
ROT_HALF = ROT_DIM // 2
CONV_WIDTH = 31
CONV_HIST = CONV_WIDTH - 1
NORM_EPS = 1e-6
PAGE = 128
LANES = 128
SUBLANES = 8
VMEM_LIMIT = 52 * 1024 * 1024
DEC_T = 4
DEC_ROWS = DEC_T * N_HEADS
NEW_LANE = PAGE - DEC_T

NT_DIMS = (((1,), (1,)), ((), ()))


def _cparams(*sem):
    return pltpu.CompilerParams(dimension_semantics=sem, vmem_limit_bytes=VMEM_LIMIT)


def _dot(a, b):
    return jnp.dot(a, b, preferred_element_type=F32)


def _dot_nt(a, b):
    return lax.dot_general(a, b, NT_DIMS, preferred_element_type=F32)


def _sigmoid(x):
    return 1.0 / (1.0 + jnp.exp(-x))


def _silu(x):
    return x * _sigmoid(x)


def _log_sigmoid(x):
    return jnp.minimum(x, 0.0) - jnp.log1p(jnp.exp(-jnp.abs(x)))


def _rms_norm(x, g):
    return x * lax.rsqrt(jnp.mean(x * x, axis=-1, keepdims=True) + NORM_EPS) * g


def _split3(x):
    p1 = x.astype(BF16)
    r1 = x - p1.astype(F32)
    p2 = r1.astype(BF16)
    r2 = r1 - p2.astype(F32)
    return p1, p2, r2.astype(BF16)


def _full(shape):
    n = len(shape)
    return pl.BlockSpec(shape, lambda *_: (0,) * n)


def _tri(n, strict, upper):
    r = lax.broadcasted_iota(jnp.int32, (n, n), 0)
    c = lax.broadcasted_iota(jnp.int32, (n, n), 1)
    if upper:
        keep = (r < c) if strict else (r <= c)
    else:
        keep = (r > c) if strict else (r >= c)
    return jnp.where(keep, 1.0, 0.0).astype(BF16)


def _row_specs(rows, nb, tm):
    per = rows // nb // tm
    row = lambda width: pl.BlockSpec((tm, width), lambda i: (i, 0))
    fmaj = lambda width: pl.BlockSpec((1, width, tm), lambda i: (i // per, 0, i % per))
    return row, fmaj


def _pre_fox_kernel(x_ref, g_ref, wq_ref, wkt_ref, wv_ref, wvt_ref, wf_ref, wft_ref, wz_ref, bf_ref, bft_ref,
                    q_ref, kt_ref, vt_ref, kbt_ref, vb_ref, lf_ref, lft_ref, gate_ref):
    h = _rms_norm(x_ref[...], g_ref[...]).astype(BF16)
    q_ref[...] = _dot(h, wq_ref[...]).astype(BF16)
    kt = _dot_nt(wkt_ref[...], h)
    kt_ref[0] = kt
    kbt_ref[0] = kt.astype(BF16)
    vt_ref[0] = _dot_nt(wvt_ref[...], h)
    v = _dot(h, wv_ref[...])
    for j in range(N_KV):
        vb_ref[j] = v[:, j * HEAD_DIM:(j + 1) * HEAD_DIM].astype(BF16)
    lf_ref[...] = _log_sigmoid(_dot(h, wf_ref[...]) + bf_ref[...])
    lft_ref[0] = _log_sigmoid(_dot_nt(wft_ref[...], h) + bft_ref[...])
    gate_ref[...] = _silu(_dot(h, wz_ref[...])).astype(BF16)


def _pre_fox(x, g, w, nb):
    rows = x.shape[0]
    tm = min(512, rows // nb)
    row, fmaj = _row_specs(rows, nb, tm)
    names = ("wq", "wkt", "wv", "wvt", "wf", "wft", "wz", "bf", "bft")
    per = rows // nb
    return pl.pallas_call(
        _pre_fox_kernel,
        grid=(rows // tm,),
        in_specs=[row(D_MODEL), _full((1, D_MODEL))] + [_full(w[n].shape) for n in names],
        out_specs=[row(D_MODEL), fmaj(KV_WIDTH), fmaj(KV_WIDTH), fmaj(KV_WIDTH),
                   pl.BlockSpec((N_KV, tm, HEAD_DIM), lambda i: (0, i, 0)),
                   row(N_HEADS), fmaj(N_HEADS), row(D_MODEL)],
        out_shape=[jax.ShapeDtypeStruct((rows, D_MODEL), BF16),
                   jax.ShapeDtypeStruct((nb, KV_WIDTH, per), F32), jax.ShapeDtypeStruct((nb, KV_WIDTH, per), F32),
                   jax.ShapeDtypeStruct((nb, KV_WIDTH, per), BF16),
                   jax.ShapeDtypeStruct((N_KV, rows, HEAD_DIM), BF16),
                   jax.ShapeDtypeStruct((rows, N_HEADS), F32), jax.ShapeDtypeStruct((nb, N_HEADS, per), F32),
                   jax.ShapeDtypeStruct((rows, D_MODEL), BF16)],
        compiler_params=_cparams("parallel"),
        name="pre_fox",
    )(x, g, *[w[n] for n in names])


def _rope_rows(u, cos, sa, sb):
    outs = []
    for c in range(u.shape[1] // LANES):
        blk = u[:, c * LANES:(c + 1) * LANES]
        outs.append(blk * cos + pltpu.roll(blk, LANES - ROT_HALF, 1) * sa + pltpu.roll(blk, ROT_HALF, 1) * sb)
    return jnp.concatenate(outs, axis=1)


def _rope_fmaj(ut, cos_t, sin_t):
    assert ROT_HALF == SUBLANES
    outs = []
    for j in range(N_KV):
        base = j * HEAD_DIM
        x1 = ut[base:base + ROT_HALF, :]
        x2 = ut[base + ROT_HALF:base + ROT_DIM, :]
        outs += [x1 * cos_t - x2 * sin_t, x2 * cos_t + x1 * sin_t, ut[base + ROT_DIM:base + HEAD_DIM, :]]
    return jnp.concatenate(outs, axis=0)


def _pre_swa_kernel(x_ref, g_ref, wq_ref, wkt_ref, wv_ref, wvt_ref, wz_ref, cos_ref, sa_ref, sb_ref,
                    cost_ref, sint_ref, q_ref, kt_ref, vt_ref, kbt_ref, vb_ref, gate_ref):
    h = _rms_norm(x_ref[...], g_ref[...]).astype(BF16)
    q_ref[...] = _rope_rows(_dot(h, wq_ref[...]), cos_ref[...], sa_ref[...], sb_ref[...]).astype(BF16)
    kt = _rope_fmaj(_dot_nt(wkt_ref[...], h), cost_ref[...], sint_ref[...])
    kt_ref[0] = kt
    kbt_ref[0] = kt.astype(BF16)
    vt_ref[0] = _dot_nt(wvt_ref[...], h)
    v = _dot(h, wv_ref[...])
    for j in range(N_KV):
        vb_ref[j] = v[:, j * HEAD_DIM:(j + 1) * HEAD_DIM].astype(BF16)
    gate_ref[...] = _silu(_dot(h, wz_ref[...])).astype(BF16)


def _pre_swa(x, g, w, tables, nb):
    rows = x.shape[0]
    tm = min(512, rows // nb)
    row, fmaj = _row_specs(rows, nb, tm)
    per = rows // nb
    nt = per // tm
    tab = pl.BlockSpec((tm, LANES), lambda i: (i % nt, 0))
    tab_t = pl.BlockSpec((ROT_HALF, tm), lambda i: (0, i % nt))
    names = ("wq", "wkt", "wv", "wvt", "wz")
    return pl.pallas_call(
        _pre_swa_kernel,
        grid=(rows // tm,),
        in_specs=[row(D_MODEL), _full((1, D_MODEL))] + [_full(w[n].shape) for n in names]
                 + [tab, tab, tab, tab_t, tab_t],
        out_specs=[row(D_MODEL), fmaj(KV_WIDTH), fmaj(KV_WIDTH), fmaj(KV_WIDTH),
                   pl.BlockSpec((N_KV, tm, HEAD_DIM), lambda i: (0, i, 0)), row(D_MODEL)],
        out_shape=[jax.ShapeDtypeStruct((rows, D_MODEL), BF16),
                   jax.ShapeDtypeStruct((nb, KV_WIDTH, per), F32), jax.ShapeDtypeStruct((nb, KV_WIDTH, per), F32),
                   jax.ShapeDtypeStruct((nb, KV_WIDTH, per), BF16),
                   jax.ShapeDtypeStruct((N_KV, rows, HEAD_DIM), BF16),
                   jax.ShapeDtypeStruct((rows, D_MODEL), BF16)],
        compiler_params=_cparams("parallel"),
        name="pre_swa",
    )(x, g, *[w[n] for n in names], *tables)


def _pre_conv_kernel(x_ref, g_ref, wa_ref, wg_ref, wz_ref, glu_ref, gate_ref):
    h = _rms_norm(x_ref[...], g_ref[...]).astype(BF16)
    glu_ref[...] = _dot(h, wa_ref[...]) * _sigmoid(_dot(h, wg_ref[...]))
    gate_ref[...] = _silu(_dot(h, wz_ref[...])).astype(BF16)


def _pre_conv(x, g, w):
    rows = x.shape[0]
    tm = min(512, rows)
    row = pl.BlockSpec((tm, D_MODEL), lambda i: (i, 0))
    return pl.pallas_call(
        _pre_conv_kernel,
        grid=(rows // tm,),
        in_specs=[row, _full((1, D_MODEL)), _full(w["wa"].shape), _full(w["wg"].shape), _full(w["wz"].shape)],
        out_specs=[row, row],
        out_shape=[jax.ShapeDtypeStruct((rows, D_MODEL), F32), jax.ShapeDtypeStruct((rows, D_MODEL), BF16)],
        compiler_params=_cparams("parallel"),
        name="pre_conv",
    )(x, g, w["wa"], w["wg"], w["wz"])


def _post_mix_kernel(*refs, gated, final):
    refs = list(refs)
    a_ref = refs.pop(0)
    gate_ref = refs.pop(0) if gated else None
    x_ref, p_ref, wo_ref, ng_ref, wg_ref, bg_ref, wp_ref = refs[:7]
    fg_ref = refs[7] if final else None
    o_ref = refs[-1]
    a = a_ref[...]
    if gated:
        a = (a * gate_ref[...].astype(F32)).astype(BF16)
    x1 = x_ref[...] + _dot(a, wo_ref[...])
    h = _rms_norm(x1, ng_ref[...]).astype(BF16)
    gate2 = _sigmoid(_dot(h, wg_ref[...]) + bg_ref[...])
    x2 = x1 + gate2 * _dot(p_ref[...].astype(BF16), wp_ref[...])
    if final:
        x2 = _rms_norm(x2, fg_ref[...])
    o_ref[...] = x2


def _post_mix(a, gate, x, p, w, final_g):
    rows = x.shape[0]
    tm = min(512, rows)
    row = lambda width: pl.BlockSpec((tm, width), lambda i: (i, 0))
    gated, final = gate is not None, final_g is not None
    pe = p.shape[1]
    args = [a] + ([gate] if gated else []) + [x, p, w["wo"], w["ng"], w["wg"], w["bg"], w["wp"]]
    specs = [row(D_MODEL)] * (2 if gated else 1)
    specs += [row(D_MODEL), row(pe), _full((D_MODEL, D_MODEL)), _full((1, D_MODEL)), _full((D_MODEL, D_MODEL)),
              _full((1, D_MODEL)), _full((pe, D_MODEL))]
    if final:
        args.append(final_g)
        specs.append(_full((1, D_MODEL)))
    return pl.pallas_call(
        functools.partial(_post_mix_kernel, gated=gated, final=final),
        grid=(rows // tm,),
        in_specs=specs,
        out_specs=row(D_MODEL),
        out_shape=jax.ShapeDtypeStruct((rows, D_MODEL), F32),
        compiler_params=_cparams("parallel"),
        name="post_mix",
    )(*args)


def _fox_cumsum_kernel(lf_ref, lft_ref, ccol_ref, crow_ref):
    nblk = lf_ref.shape[0] // LANES
    low = _tri(LANES, strict=False, upper=False)
    upp = _tri(LANES, strict=False, upper=True)

    def body(n, carry):
        cc, cr = carry
        off = pl.multiple_of(n * LANES, LANES)
        a1, a2, a3 = _split3(lf_ref[pl.ds(off, LANES), :])
        c = _dot(low, a1) + _dot(low, a2) + _dot(low, a3) + cc
        ccol_ref[pl.ds(off, LANES), :] = c
        b1, b2, b3 = _split3(lft_ref[0, :, pl.ds(off, LANES)])
        ct = _dot(b1, upp) + _dot(b2, upp) + _dot(b3, upp) + cr
        crow_ref[0, :, pl.ds(off, LANES)] = ct
        return c[LANES - 1:LANES, :], ct[:, LANES - 1:LANES]

    lax.fori_loop(0, nblk, body, (jnp.zeros((1, N_HEADS), F32), jnp.zeros((N_HEADS, 1), F32)))


def _fox_cumsum(lf, lft):
    batch, _, seq = lft.shape
    col = pl.BlockSpec((seq, N_HEADS), lambda b: (b, 0))
    rowm = pl.BlockSpec((1, N_HEADS, seq), lambda b: (b, 0, 0))
    return pl.pallas_call(
        _fox_cumsum_kernel,
        grid=(batch,),
        in_specs=[col, rowm],
        out_specs=[col, rowm],
        out_shape=[jax.ShapeDtypeStruct(lf.shape, F32), jax.ShapeDtypeStruct(lft.shape, F32)],
        compiler_params=_cparams("parallel"),
        name="fox_cumsum",
    )(lf, lft)


def _fox_prompt_kernel(q_ref, kt_ref, v_ref, ccol_ref, crow_ref, gate_ref, o_ref, *, tq):
    i = pl.program_id(2)
    causal = (lax.broadcasted_iota(jnp.int32, (tq, tq), 0) >= lax.broadcasted_iota(jnp.int32, (tq, tq), 1))
    outs = []
    for g in range(GROUP):
        q = q_ref[:, g * HEAD_DIM:(g + 1) * HEAD_DIM]
        cq = ccol_ref[0, :, g:g + 1]

        def step(j, carry, masked, q=q, cq=cq, g=g):
            m, l, acc = carry
            off = pl.multiple_of(j * tq, tq)
            s = _dot(q, kt_ref[0, :, pl.ds(off, tq)])
            s = s + cq - crow_ref[0, 0, g:g + 1, pl.ds(off, tq)]
            if masked:
                s = jnp.where(causal, s, -jnp.inf)
            m_new = jnp.maximum(m, jnp.max(s, axis=-1, keepdims=True))
            alpha = jnp.exp(m - m_new)
            p = jnp.exp(s - m_new)
            l = alpha * l + jnp.sum(p, axis=-1, keepdims=True)
            acc = alpha * acc + _dot(p.astype(BF16), v_ref[0, pl.ds(off, tq), :])
            return m_new, l, acc

        init = (jnp.full((tq, 1), -jnp.inf, F32), jnp.zeros((tq, 1), F32), jnp.zeros((tq, HEAD_DIM), F32))
        carry = lax.fori_loop(0, i, functools.partial(step, masked=False), init)
        _, l, acc = step(i, carry, True)
        outs.append(acc / l)
    o = jnp.concatenate(outs, axis=1)
    o_ref[...] = (o * gate_ref[...].astype(F32)).astype(BF16)


def _fox_prompt(q, kbt, vb, ccol, crow, gate):
    batch, _, seq = kbt.shape
    rows = batch * seq
    tq = min(256, seq)
    nq = seq // tq
    width = GROUP * HEAD_DIM
    qspec = pl.BlockSpec((tq, width), lambda b, h, i: (b * nq + i, h))
    return pl.pallas_call(
        functools.partial(_fox_prompt_kernel, tq=tq),
        grid=(batch, N_KV, nq),
        in_specs=[qspec,
                  pl.BlockSpec((1, HEAD_DIM, seq), lambda b, h, i: (b, h, 0)),
                  pl.BlockSpec((1, seq, HEAD_DIM), lambda b, h, i: (h, b, 0)),
                  pl.BlockSpec((1, tq, GROUP), lambda b, h, i: (h, b * nq + i, 0)),
                  pl.BlockSpec((1, 1, GROUP, seq), lambda b, h, i: (b, h, 0, 0)),
                  qspec],
        out_specs=qspec,
        out_shape=jax.ShapeDtypeStruct((rows, D_MODEL), BF16),
        compiler_params=_cparams("parallel", "parallel", "arbitrary"),
        name="fox_prompt",
    )(q, kbt, vb, ccol, crow, gate)


def _block_diag_mask():
    r = lax.broadcasted_iota(jnp.int32, (DEC_ROWS, KV_WIDTH), 0)
    c = lax.broadcasted_iota(jnp.int32, (DEC_ROWS, KV_WIDTH), 1)
    return ((r % N_HEADS) // GROUP) == (c // HEAD_DIM)


def _fold_heads(o_full):
    om = jnp.where(_block_diag_mask(), o_full, 0.0)
    return (om[:, 0:HEAD_DIM] + om[:, HEAD_DIM:2 * HEAD_DIM]
            + om[:, 2 * HEAD_DIM:3 * HEAD_DIM] + om[:, 3 * HEAD_DIM:4 * HEAD_DIM])


def _new_token_mask():
    lane = lax.broadcasted_iota(jnp.int32, (DEC_ROWS, PAGE), 1)
    t = lax.broadcasted_iota(jnp.int32, (DEC_ROWS, PAGE), 0) // N_HEADS
    return (lane >= NEW_LANE) & (lane - NEW_LANE <= t)


def _fox_decode_kernel(pt_ref, qbd_ref, knew_ref, vnew_ref, lfnew_ref, kc_ref, vc_ref, lc_ref, o_ref,
                       kbuf, vbuf, lbuf, sems, *, layer, cp):
    b = pl.program_id(0)
    nb = pl.num_programs(0)
    nch = pt_ref.shape[1] // cp

    def copies(seq, ci, slot):
        base = (nch - 1 - ci) * cp
        out = []
        for p in range(cp):
            page = pt_ref[seq, base + p]
            out.append(pltpu.make_async_copy(kc_ref.at[layer, page], kbuf.at[slot, p], sems.at[0, slot]))
            out.append(pltpu.make_async_copy(vc_ref.at[layer, page], vbuf.at[slot, p], sems.at[1, slot]))
            out.append(pltpu.make_async_copy(lc_ref.at[layer, page], lbuf.at[slot, p], sems.at[2, slot]))
        return out

    @pl.when(b == 0)
    def _():
        for c in copies(0, 0, 0):
            c.start()

    qbd = qbd_ref[0]

    x = lfnew_ref[0]
    cn = x + pltpu.roll(x, 1, 1) + pltpu.roll(x, 2, 1) + pltpu.roll(x, 3, 1)
    s_new = _dot(qbd, knew_ref[0].astype(BF16))
    cqs = [cn[:, NEW_LANE + t:NEW_LANE + t + 1] for t in range(DEC_T)]
    s_new = s_new + jnp.concatenate([cqs[t] - cn for t in range(DEC_T)], axis=0)
    s_new = jnp.where(_new_token_mask(), s_new, -jnp.inf)
    cq = jnp.concatenate(cqs, axis=0)
    m = jnp.max(s_new, axis=-1, keepdims=True)
    p = jnp.exp(s_new - m)
    l = jnp.sum(p, axis=-1, keepdims=True)
    acc = _dot_nt(p.astype(BF16), vnew_ref[0].astype(BF16))

    sfx = jnp.concatenate([_tri(PAGE, strict=True, upper=False), jnp.ones((PAGE, LANES), BF16)], axis=1)

    def chunk(ci, carry):
        m, l, acc, later = carry
        slot = (b * nch + ci) % 2

        @pl.when(ci + 1 < nch)
        def _():
            for c in copies(b, ci + 1, 1 - slot):
                c.start()

        @pl.when(jnp.logical_and(ci + 1 == nch, b + 1 < nb))
        def _():
            for c in copies(b + 1, 0, 1 - slot):
                c.start()

        for c in copies(b, ci, slot):
            c.wait()

        q1, q2, q3 = _split3(lbuf[slot].reshape(cp * N_HEADS, PAGE))
        st = (_dot(q1, sfx) + _dot(q2, sfx) + _dot(q3, sfx)).reshape(cp, N_HEADS, 2 * LANES)
        pieces = [None] * cp
        for pg in reversed(range(cp)):
            pieces[pg] = st[pg, :, :LANES] + later
            later = later + st[pg, :, LANES:]
        s = jnp.concatenate([_dot(qbd, kbuf[slot, pg].astype(BF16)) for pg in range(cp)], axis=1)
        bias = jnp.concatenate(pieces, axis=1)
        s = s + jnp.concatenate([bias] * DEC_T, axis=0) + cq
        m_new = jnp.maximum(m, jnp.max(s, axis=-1, keepdims=True))
        alpha = jnp.exp(m - m_new)
        p = jnp.exp(s - m_new)
        l = alpha * l + jnp.sum(p, axis=-1, keepdims=True)
        p = p.astype(BF16)
        acc = alpha * acc
        for pg in range(cp):
            acc = acc + _dot_nt(p[:, pg * PAGE:(pg + 1) * PAGE], vbuf[slot, pg].astype(BF16))
        return m_new, l, acc, later

    m, l, acc, _ = lax.fori_loop(0, nch, chunk, (m, l, acc, jnp.zeros((N_HEADS, LANES), F32)))
    o_ref[0] = _fold_heads(acc / l)


def _fox_decode(page_table, qbd, knew, vnew, lfnew, cache_kt, cache_vt, cache_lft, layer):
    batch, n_pages = page_table.shape
    cp = 16
    while n_pages % cp:
        cp //= 2
    seq3 = lambda r, c: pl.BlockSpec((1, r, c), lambda b, pt: (b, 0, 0))
    grid_spec = pltpu.PrefetchScalarGridSpec(
        num_scalar_prefetch=1,
        grid=(batch,),
        in_specs=[seq3(DEC_ROWS, KV_WIDTH), seq3(KV_WIDTH, PAGE), seq3(KV_WIDTH, PAGE), seq3(N_HEADS, PAGE),
                  pl.BlockSpec(memory_space=pl.ANY), pl.BlockSpec(memory_space=pl.ANY),
                  pl.BlockSpec(memory_space=pl.ANY)],
        out_specs=seq3(DEC_ROWS, HEAD_DIM),
        scratch_shapes=[pltpu.VMEM((2, cp, KV_WIDTH, PAGE), F32), pltpu.VMEM((2, cp, KV_WIDTH, PAGE), F32),
                        pltpu.VMEM((2, cp, N_HEADS, PAGE), F32), pltpu.SemaphoreType.DMA((3, 2))],
    )
    return pl.pallas_call(
        functools.partial(_fox_decode_kernel, layer=layer, cp=cp),
        grid_spec=grid_spec,
        out_shape=jax.ShapeDtypeStruct((batch, DEC_ROWS, HEAD_DIM), F32),
        compiler_params=_cparams("arbitrary"),
        name="fox_decode",
    )(page_table, qbd, knew, vnew, lfnew, cache_kt, cache_vt, cache_lft)


def _swa_prompt_kernel(q_ref, kp_ref, kc_ref, vp_ref, vc_ref, sink_ref, gate_ref, o_ref):
    n = pl.program_id(2)
    tq = q_ref.shape[0]
    kk = jnp.concatenate([kp_ref[0], kc_ref[0]], axis=1)
    vv = jnp.concatenate([vp_ref[0], vc_ref[0]], axis=0)
    qi = lax.broadcasted_iota(jnp.int32, (tq, 2 * tq), 0)
    si = lax.broadcasted_iota(jnp.int32, (tq, 2 * tq), 1)
    rel = tq + qi - si
    allowed = (rel >= 0) & (rel < WINDOW) & ((si >= tq) | (n > 0))
    outs = []
    for g in range(GROUP):
        s = _dot(q_ref[:, g * HEAD_DIM:(g + 1) * HEAD_DIM], kk)
        s = jnp.where(allowed, s, -jnp.inf)
        sink = sink_ref[0, :, g:g + 1]
        m = jnp.maximum(jnp.max(s, axis=-1, keepdims=True), sink)
        p = jnp.exp(s - m)
        denom = jnp.sum(p, axis=-1, keepdims=True) + jnp.exp(sink - m)
        outs.append(_dot(p.astype(BF16), vv) / denom)
    o = jnp.concatenate(outs, axis=1)
    o_ref[...] = (o * gate_ref[...].astype(F32)).astype(BF16)


def _swa_prompt(q, kbt, vb, sinks, gate):
    batch, _, seq = kbt.shape
    rows = batch * seq
    tq = Q_BLOCK
    nb = seq // tq
    width = GROUP * HEAD_DIM
    qspec = pl.BlockSpec((tq, width), lambda b, h, n: (b * nb + n, h))
    kprev = pl.BlockSpec((1, HEAD_DIM, tq), lambda b, h, n: (b, h, jnp.maximum(n - 1, 0)))
    kcur = pl.BlockSpec((1, HEAD_DIM, tq), lambda b, h, n: (b, h, n))
    vprev = pl.BlockSpec((1, tq, HEAD_DIM), lambda b, h, n: (h, jnp.maximum(b * nb + n - 1, 0), 0))
    vcur = pl.BlockSpec((1, tq, HEAD_DIM), lambda b, h, n: (h, b * nb + n, 0))
    return pl.pallas_call(
        _swa_prompt_kernel,
        grid=(batch, N_KV, nb),
        in_specs=[qspec, kprev, kcur, vprev, vcur, pl.BlockSpec((1, 1, GROUP), lambda b, h, n: (h, 0, 0)), qspec],
        out_specs=qspec,
        out_shape=jax.ShapeDtypeStruct((rows, D_MODEL), BF16),
        compiler_params=_cparams("parallel", "parallel", "arbitrary"),
        name="swa_prompt",
    )(q, kbt, kbt, vb, vb, sinks.reshape(N_KV, 1, GROUP), gate)


def _swa_decode_kernel(qbd_ref, knew_ref, vnew_ref, sk_ref, sv_ref, sink_ref, o_ref, nk_ref, nv_ref):
    qbd = qbd_ref[0]
    sk, sv, kn, vn = sk_ref[0, 0], sv_ref[0, 0], knew_ref[0], vnew_ref[0]
    lane = lax.broadcasted_iota(jnp.int32, (DEC_ROWS, PAGE), 1)
    t = lax.broadcasted_iota(jnp.int32, (DEC_ROWS, PAGE), 0) // N_HEADS
    s_old = jnp.where(lane > t, _dot(qbd, sk.astype(BF16)), -jnp.inf)
    s_new = jnp.where(_new_token_mask(), _dot(qbd, kn.astype(BF16)), -jnp.inf)
    sink = sink_ref[...]
    m = jnp.maximum(jnp.maximum(jnp.max(s_old, axis=-1, keepdims=True), jnp.max(s_new, axis=-1, keepdims=True)),
                    sink)
    p_old = jnp.exp(s_old - m)
    p_new = jnp.exp(s_new - m)
    denom = (jnp.sum(p_old, axis=-1, keepdims=True) + jnp.sum(p_new, axis=-1, keepdims=True)
             + jnp.exp(sink - m))
    o_full = _dot_nt(p_old.astype(BF16), sv.astype(BF16)) + _dot_nt(p_new.astype(BF16), vn.astype(BF16))
    o_ref[0] = _fold_heads(o_full / denom)
    keep_new = lax.broadcasted_iota(jnp.int32, (KV_WIDTH, PAGE), 1) >= NEW_LANE
    nk_ref[0] = jnp.where(keep_new, kn, pltpu.roll(sk, NEW_LANE, 1))
    nv_ref[0] = jnp.where(keep_new, vn, pltpu.roll(sv, NEW_LANE, 1))


def _swa_decode(qbd, knew, vnew, state_kt, state_vt, sink_rows, layer):
    batch, buf = state_kt.shape[1], state_kt.shape[3]
    assert buf == WINDOW == PAGE
    seq3 = lambda r, c: pl.BlockSpec((1, r, c), lambda b: (b, 0, 0))
    st = pl.BlockSpec((1, 1, KV_WIDTH, buf), lambda b: (layer, b, 0, 0))
    return pl.pallas_call(
        _swa_decode_kernel,
        grid=(batch,),
        in_specs=[seq3(DEC_ROWS, KV_WIDTH), seq3(KV_WIDTH, PAGE), seq3(KV_WIDTH, PAGE), st, st,
                  _full((DEC_ROWS, 1))],
        out_specs=[seq3(DEC_ROWS, HEAD_DIM), seq3(KV_WIDTH, buf), seq3(KV_WIDTH, buf)],
        out_shape=[jax.ShapeDtypeStruct((batch, DEC_ROWS, HEAD_DIM), F32),
                   jax.ShapeDtypeStruct((batch, KV_WIDTH, buf), F32),
                   jax.ShapeDtypeStruct((batch, KV_WIDTH, buf), F32)],
        compiler_params=_cparams("parallel"),
        name="swa_decode",
    )(qbd, knew, vnew, state_kt, state_vt, sink_rows)


HALO = 32


def _conv_tail(y, gate, lng, lnb):
    yc = y - jnp.mean(y, axis=-1, keepdims=True)
    yn = yc * lax.rsqrt(jnp.mean(yc * yc, axis=-1, keepdims=True) + NORM_EPS) * lng + lnb
    return (_silu(yn) * gate.astype(F32)).astype(BF16)


def _conv_prompt_kernel(x_ref, g_ref, wa_ref, wg_ref, wz_ref, dw_ref, db_ref, lng_ref, lnb_ref,
                        a_ref, st_ref, buf):
    i = pl.program_id(1)
    tm = x_ref.shape[0]
    h = _rms_norm(x_ref[...], g_ref[...]).astype(BF16)
    glu = _dot(h, wa_ref[...]) * _sigmoid(_dot(h, wg_ref[...]))
    gate = _silu(_dot(h, wz_ref[...]))

    @pl.when(i == 0)
    def _():
        buf[0:HALO, :] = jnp.zeros((HALO, D_MODEL), F32)

    buf[HALO:HALO + tm, :] = glu
    first = HALO - CONV_HIST
    y = jnp.zeros((tm, D_MODEL), F32) + db_ref[...]
    for w in range(CONV_WIDTH):
        y = y + buf[first + w:first + w + tm, :] * dw_ref[w:w + 1, :]
    a_ref[...] = _conv_tail(y, gate, lng_ref[...], lnb_ref[...])

    @pl.when(i == pl.num_programs(1) - 1)
    def _():
        st_ref[0] = buf[HALO + tm - CONV_HIST:HALO + tm, :]

    buf[0:HALO, :] = buf[tm:tm + HALO, :]


def _conv_prompt(x, g, w, batch, seq):
    rows = batch * seq
    tm = min(256, seq)
    nt = seq // tm
    row = pl.BlockSpec((tm, D_MODEL), lambda b, i: (b * nt + i, 0))
    return pl.pallas_call(
        _conv_prompt_kernel,
        grid=(batch, nt),
        in_specs=[row, _full((1, D_MODEL)), _full(w["wa"].shape), _full(w["wg"].shape), _full(w["wz"].shape),
                  _full((CONV_WIDTH, D_MODEL)), _full((1, D_MODEL)), _full((1, D_MODEL)), _full((1, D_MODEL))],
        out_specs=[row, pl.BlockSpec((1, CONV_HIST, D_MODEL), lambda b, i: (b, 0, 0))],
        out_shape=[jax.ShapeDtypeStruct((rows, D_MODEL), BF16),
                   jax.ShapeDtypeStruct((batch, CONV_HIST, D_MODEL), F32)],
        scratch_shapes=[pltpu.VMEM((HALO + tm, D_MODEL), F32)],
        compiler_params=_cparams("arbitrary", "arbitrary"),
        name="conv_prompt",
    )(x, g, w["wa"], w["wg"], w["wz"], w["dw"], w["db"], w["lng"], w["lnb"])


def _conv_decode_kernel(glu_ref, gate_ref, st_ref, dw_ref, db_ref, lng_ref, lnb_ref, a_ref, ns_ref, pad):
    sb = glu_ref.shape[1]
    pad[0:CONV_HIST] = st_ref[0]
    pad[CONV_HIST:CONV_HIST + DEC_T] = glu_ref[...]
    y = jnp.zeros((DEC_T, sb, D_MODEL), F32) + db_ref[...]
    for w in range(CONV_WIDTH):
        y = y + pad[w:w + DEC_T] * dw_ref[w:w + 1, :]
    a_ref[...] = _conv_tail(y, gate_ref[...], lng_ref[...], lnb_ref[...])
    ns_ref[...] = pad[DEC_T:DEC_T + CONV_HIST]


def _conv_decode(glu, gate, state, w, layer):
    batch = state.shape[2]
    sb = 16
    while batch % sb:
        sb //= 2
    tok = pl.BlockSpec((DEC_T, sb, D_MODEL), lambda i: (0, i, 0))
    return pl.pallas_call(
        _conv_decode_kernel,
        grid=(batch // sb,),
        in_specs=[tok, tok, pl.BlockSpec((1, CONV_HIST, sb, D_MODEL), lambda i: (layer, 0, i, 0)),
                  _full((CONV_WIDTH, D_MODEL)), _full((1, D_MODEL)), _full((1, D_MODEL)), _full((1, D_MODEL))],
        out_specs=[tok, pl.BlockSpec((CONV_HIST, sb, D_MODEL), lambda i: (0, i, 0))],
        out_shape=[jax.ShapeDtypeStruct((DEC_T, batch, D_MODEL), BF16),
                   jax.ShapeDtypeStruct((CONV_HIST, batch, D_MODEL), F32)],
        scratch_shapes=[pltpu.VMEM((CONV_HIST + DEC_T, sb, D_MODEL), F32)],
        compiler_params=_cparams("parallel"),
        name="conv_decode",
    )(glu.reshape(DEC_T, batch, D_MODEL), gate.reshape(DEC_T, batch, D_MODEL), state,
      w["dw"], w["db"], w["lng"], w["lnb"])


def _rope_tables(pos):
    inv_freq = ROPE_THETA ** (-jnp.arange(ROT_HALF, dtype=F32) / ROT_HALF)
    ang = pos.astype(F32)[:, None] * inv_freq[None, :]
    cos, sin = jnp.cos(ang), jnp.sin(ang)
    n = pos.shape[0]
    rest = jnp.zeros((n, HEAD_DIM - ROT_DIM), F32)
    zero = jnp.zeros((n, ROT_HALF), F32)
    cos_h = jnp.concatenate([cos, cos, rest + 1.0], axis=1)
    sa_h = jnp.concatenate([-sin, zero, rest], axis=1)
    sb_h = jnp.concatenate([zero, sin, rest], axis=1)
    rep = LANES // HEAD_DIM
    return tuple(jnp.tile(t, (1, rep)) for t in (cos_h, sa_h, sb_h)) + (cos.T, sin.T)


def _decode_queries(q_rows, batch):
    q4 = q_rows.reshape(DEC_T, batch, N_HEADS, 1, HEAD_DIM).transpose(1, 0, 2, 3, 4)
    own = (jnp.arange(N_HEADS)[:, None] // GROUP) == jnp.arange(N_KV)[None, :]
    return jnp.where(own[None, None, :, :, None], q4, jnp.zeros((), q_rows.dtype)).reshape(batch, DEC_ROWS, KV_WIDTH)


def _new_token_pages(xt, batch):
    width = xt.shape[1]
    per_seq = xt.reshape(width, DEC_T, batch).transpose(2, 0, 1)
    return jnp.pad(per_seq, ((0, 0), (0, 0), (NEW_LANE, 0)))


def _decode_out_rows(o, batch):
    return o.reshape(batch, DEC_T, D_MODEL).transpose(1, 0, 2).reshape(DEC_T * batch, D_MODEL)


def _sample_major(xt, batch, inner):
    return xt.reshape(inner + (DEC_T, batch)).transpose((len(inner) + 1, len(inner)) + tuple(range(len(inner))))


def kernel(x_prompt, x_sample, cache_fox_k, cache_fox_v, cache_fox_lf, page_table, state_swa_k, state_swa_v,
           state_conv, p_prompt, p_sample, norm_g, fox_w_in, fox_b_f, fox_w_out, swa_w_in, swa_sinks, swa_w_out,
           conv_w_in, conv_dw_w, conv_dw_b, conv_ln_g, conv_ln_b, conv_w_out, pe_w_proj, pe_norm_g, pe_w_gate,
           pe_b_gate, final_norm_g):
    batch, seq, _ = x_prompt.shape
    dec_batch, dec_t, _ = x_sample.shape
    assert dec_t == DEC_T and seq % Q_BLOCK == 0
    depth = norm_g.shape[0]
    past_len = page_table.shape[1] * PAGE
    buf = state_swa_k.shape[2]
    rows_p, rows_s = batch * seq, dec_batch * dec_t
    att = N_HEADS * HEAD_DIM

    xp = x_prompt.reshape(rows_p, D_MODEL)
    xs = x_sample.transpose(1, 0, 2).reshape(rows_s, D_MODEL)
    fmaj = lambda c: c.transpose(0, 1, 3, 4, 2).reshape(c.shape[:2] + (KV_WIDTH, c.shape[2]))
    cache_kt, cache_vt = fmaj(cache_fox_k), fmaj(cache_fox_v)
    cache_lft = cache_fox_lf.transpose(0, 1, 3, 2)
    st_kt, st_vt = fmaj(state_swa_k), fmaj(state_swa_v)
    st_conv = state_conv.transpose(0, 2, 1, 3)
    heads_major = lambda t: t.reshape(t.shape[0], N_KV, HEAD_DIM, t.shape[2]).transpose(0, 3, 1, 2)

    fox_k_p, fox_v_p, fox_lf_p, fox_k_s, fox_v_s, fox_lf_s = [], [], [], [], [], []
    swa_k_p, swa_v_p, swa_k_s, swa_v_s = [], [], [], []
    conv_p, conv_s = [], []

    for i in range(depth):
        kind, j = i % N_MIXERS, i // N_MIXERS
        g = norm_g[i].reshape(1, D_MODEL)
        if kind == 0:
            w_in = fox_w_in[j]
            wk = w_in[:, att:att + KV_WIDTH]
            wv = w_in[:, att + KV_WIDTH:att + 2 * KV_WIDTH]
            wf = w_in[:, att + 2 * KV_WIDTH:att + 2 * KV_WIDTH + N_HEADS]
            w = dict(wq=(w_in[:, :att] * ATT_SCALE).astype(BF16), wkt=wk.T.astype(BF16),
                     wv=wv.astype(BF16), wvt=wv.T.astype(BF16), wf=wf.astype(BF16), wft=wf.T.astype(BF16),
                     wz=w_in[:, att + 2 * KV_WIDTH + N_HEADS:].astype(BF16),
                     bf=fox_b_f[j].reshape(1, N_HEADS), bft=fox_b_f[j].reshape(N_HEADS, 1))
            w_out = fox_w_out[j]
            q, kt, vt, kbt, vb, lf, lft, gate = _pre_fox(xp, g, w, batch)
            ccol, crow = _fox_cumsum(lf, lft)
            ccol = ccol.reshape(rows_p, N_KV, GROUP).transpose(1, 0, 2)
            crow = crow.reshape(batch, N_KV, GROUP, seq)
            ap = _fox_prompt(q, kbt, vb, ccol, crow, gate)
            gate_p = None
            fox_k_p.append(heads_major(kt))
            fox_v_p.append(heads_major(vt))
            fox_lf_p.append(lft.transpose(0, 2, 1))
            q, kt, vt, _, _, _, lft, gate_s = _pre_fox(xs, g, w, 1)
            o = _fox_decode(page_table, _decode_queries(q, dec_batch), _new_token_pages(kt, dec_batch),
                            _new_token_pages(vt, dec_batch), _new_token_pages(lft, dec_batch),
                            cache_kt, cache_vt, cache_lft, j)
            a_s = _decode_out_rows(o, dec_batch)
            fox_k_s.append(_sample_major(kt, dec_batch, (N_KV, HEAD_DIM)))
            fox_v_s.append(_sample_major(vt, dec_batch, (N_KV, HEAD_DIM)))
            fox_lf_s.append(_sample_major(lft, dec_batch, (N_HEADS,)))
        elif kind == 1:
            w_in = swa_w_in[j]
            wk = w_in[:, att:att + KV_WIDTH]
            wv = w_in[:, att + KV_WIDTH:att + 2 * KV_WIDTH]
            w = dict(wq=(w_in[:, :att] * ATT_SCALE).astype(BF16), wkt=wk.T.astype(BF16),
                     wv=wv.astype(BF16), wvt=wv.T.astype(BF16), wz=w_in[:, att + 2 * KV_WIDTH:].astype(BF16))
            w_out = swa_w_out[j]
            q, kt, vt, kbt, vb, gate = _pre_swa(xp, g, w, _rope_tables(jnp.arange(seq, dtype=jnp.int32)), batch)
            ap = _swa_prompt(q, kbt, vb, swa_sinks[j], gate)
            gate_p = None
            swa_k_p.append(heads_major(kt[:, :, seq - buf:]))
            swa_v_p.append(heads_major(vt[:, :, seq - buf:]))
            pos_s = past_len + jnp.repeat(jnp.arange(DEC_T, dtype=jnp.int32), dec_batch)
            q, kt, vt, _, _, gate_s = _pre_swa(xs, g, w, _rope_tables(pos_s), 1)
            sink_rows = jnp.tile(swa_sinks[j].astype(F32), DEC_T).reshape(DEC_ROWS, 1)
            o, nkt, nvt = _swa_decode(_decode_queries(q, dec_batch), _new_token_pages(kt, dec_batch),
                                      _new_token_pages(vt, dec_batch), st_kt, st_vt, sink_rows, j)
            a_s = _decode_out_rows(o, dec_batch)
            swa_k_s.append(heads_major(nkt))
            swa_v_s.append(heads_major(nvt))
        else:
            w_in = conv_w_in[j]
            cd = w_in.shape[1] // 3
            w = dict(wa=w_in[:, :cd].astype(BF16), wg=w_in[:, cd:2 * cd].astype(BF16),
                     wz=w_in[:, 2 * cd:].astype(BF16), dw=conv_dw_w[j], db=conv_dw_b[j].reshape(1, cd),
                     lng=conv_ln_g[j].reshape(1, cd), lnb=conv_ln_b[j].reshape(1, cd))
            w_out = conv_w_out[j]
            ap, st = _conv_prompt(xp, g, w, batch, seq)
            gate_p = None
            conv_p.append(st)
            glu, gate = _pre_conv(xs, g, w)
            a_s, ns = _conv_decode(glu, gate, st_conv, w, j)
            a_s = a_s.reshape(rows_s, D_MODEL)
            gate_s = None
            conv_s.append(ns.transpose(1, 0, 2))
        wpost = dict(wo=w_out.astype(BF16), ng=pe_norm_g[i].reshape(1, D_MODEL), wg=pe_w_gate[i].astype(BF16),
                     bg=pe_b_gate[i].reshape(1, D_MODEL), wp=pe_w_proj[i].astype(BF16))
        fg = final_norm_g.reshape(1, D_MODEL) if i == depth - 1 else None
        xp = _post_mix(ap, gate_p, xp, p_prompt[i].reshape(rows_p, -1), wpost, fg)
        xs = _post_mix(a_s, gate_s, xs, p_sample[i].transpose(1, 0, 2).reshape(rows_s, -1), wpost, fg)

    return (xp.reshape(batch, seq, D_MODEL), xs.reshape(DEC_T, dec_batch, D_MODEL).transpose(1, 0, 2),
            jnp.stack(fox_k_p), jnp.stack(fox_v_p), jnp.stack(fox_lf_p),
            jnp.stack(fox_k_s), jnp.stack(fox_v_s), jnp.stack(fox_lf_s),
            jnp.stack(swa_k_p), jnp.stack(swa_v_p), jnp.stack(swa_k_s), jnp.stack(swa_v_s),
            jnp.stack(conv_p), jnp.stack(conv_s))
```

```python
import functools

import numpy as np

import jax
import jax.numpy as jnp
from jax import lax
from jax.experimental import pallas as pl
from jax.experimental.pallas import tpu as pltpu

F32 = jnp.float32
BF16 = jnp.bfloat16

D_MODEL = 1024
HEAD_DIM = 64
N_HEADS = 16
N_KV = 4
GROUP = 4
KV_WIDTH = N_KV * HEAD_DIM
ATT_SCALE = HEAD_DIM ** -0.5
N_MIXERS = 3
Q_BLOCK = 128
WINDOW = 128
ROPE_THETA = 500000.0
ROT_DIM = HEAD_DIM // 4
ROT_HALF = ROT_DIM // 2
CONV_WIDTH = 31
CONV_HIST = CONV_WIDTH - 1
NORM_EPS = 1e-6
PAGE = 128
LANES = 128
SUBLANES = 8
VMEM_LIMIT = 52 * 1024 * 1024
DEC_T = 4
DEC_ROWS = DEC_T * N_HEADS
NEW_LANE = PAGE - DEC_T
DEC_GROUPS = 4

QK_AUG = 128
BIAS_SLOTS = 8
N_PIECES = 3
V_AUG = 80
Q_AUG_ROWS = N_HEADS * QK_AUG
K_AUG_LANES = N_KV * QK_AUG

NT_DIMS = (((1,), (1,)), ((), ()))


def _cparams(*sem, flags=None):
    return pltpu.CompilerParams(dimension_semantics=sem, vmem_limit_bytes=VMEM_LIMIT, flags=flags)


def _dot(a, b):
    return jnp.dot(a, b, preferred_element_type=F32)


def _dot_nt(a, b):
    return lax.dot_general(a, b, NT_DIMS, preferred_element_type=F32)


def _sigmoid(x):
    return 1.0 / (1.0 + jnp.exp(-x))


def _silu(x):
    return x * _sigmoid(x)


def _log_sigmoid(x):
    return jnp.minimum(x, 0.0) - jnp.log1p(jnp.exp(-jnp.abs(x)))


def _rms_norm(x, g):
    return x * lax.rsqrt(jnp.mean(x * x, axis=-1, keepdims=True) + NORM_EPS) * g


def _split3(x):
    p1 = x.astype(BF16)
    r1 = x - p1.astype(F32)
    p2 = r1.astype(BF16)
    r2 = r1 - p2.astype(F32)
    return p1, p2, r2.astype(BF16)


def _dot3(pieces, m):
    return _dot(pieces[0], m) + _dot(pieces[1], m) + _dot(pieces[2], m)


def _full(shape):
    n = len(shape)
    return pl.BlockSpec(shape, lambda *_: (0,) * n)


def _tri(n, strict, upper):
    r = lax.broadcasted_iota(jnp.int32, (n, n), 0)
    c = lax.broadcasted_iota(jnp.int32, (n, n), 1)
    if upper:
        keep = (r < c) if strict else (r <= c)
    else:
        keep = (r > c) if strict else (r >= c)
    return jnp.where(keep, 1.0, 0.0).astype(BF16)


def _row_specs(rows, nb, tm):
    per = rows // nb // tm
    row = lambda width: pl.BlockSpec((tm, width), lambda i: (i, 0))
    fmaj = lambda width: pl.BlockSpec((1, width, tm), lambda i: (i // per, 0, i % per))
    return row, fmaj


def _rope_rows(u, cos, sa, sb):
    outs = []
    for c in range(u.shape[1] // LANES):
        blk = u[:, c * LANES:(c + 1) * LANES]
        outs.append(blk * cos + pltpu.roll(blk, LANES - ROT_HALF, 1) * sa + pltpu.roll(blk, ROT_HALF, 1) * sb)
    return jnp.concatenate(outs, axis=1)


def _rope_fmaj(ut, cos_t, sin_t, stride):
    assert ROT_HALF == SUBLANES
    outs = []
    for base in range(0, ut.shape[0], stride):
        x1 = ut[base:base + ROT_HALF, :]
        x2 = ut[base + ROT_HALF:base + ROT_DIM, :]
        outs += [x1 * cos_t - x2 * sin_t, x2 * cos_t + x1 * sin_t, ut[base + ROT_DIM:base + stride, :]]
    return jnp.concatenate(outs, axis=0)


def _store_q_aug(qta_ref, qt, bias_pieces):
    tm = qt.shape[1]
    sub = lax.broadcasted_iota(jnp.int32, (BIAS_SLOTS, tm), 0)
    ones = jnp.where((sub >= N_PIECES) & (sub < 2 * N_PIECES), 1.0, 0.0)
    for head in range(N_HEADS):
        parts = [qt[head * HEAD_DIM:(head + 1) * HEAD_DIM, :]]
        if bias_pieces is None:
            parts.append(jnp.zeros((QK_AUG - HEAD_DIM, tm), F32))
        else:
            g = head % GROUP
            grp = ones
            for n in reversed(range(N_PIECES)):
                grp = jnp.where(sub == n, bias_pieces[n][head:head + 1, :], grp)
            before, after = BIAS_SLOTS * g, QK_AUG - HEAD_DIM - BIAS_SLOTS * (g + 1)
            parts += ([jnp.zeros((before, tm), F32)] if before else []) + [grp]
            parts += [jnp.zeros((after, tm), F32)] if after else []
        qta_ref[0, head * QK_AUG:(head + 1) * QK_AUG, :] = jnp.concatenate(parts, axis=0).astype(BF16)


def _store_v_aug(vta_ref, vt):
    tm = vt.shape[1]
    pad = V_AUG - HEAD_DIM
    ones_row = jnp.where(lax.broadcasted_iota(jnp.int32, (pad, tm), 0) == 0, 1.0, 0.0)
    for j in range(N_KV):
        blk = jnp.concatenate([vt[j * HEAD_DIM:(j + 1) * HEAD_DIM, :], ones_row], axis=0)
        vta_ref[0, j * V_AUG:(j + 1) * V_AUG, :] = blk.astype(BF16)


def _pre_fox_prompt_kernel(x_ref, g_ref, wqt_ref, wk_ref, wkt_ref, wvt_ref, wf_ref, wft_ref, wz_ref,
                           bf_ref, bft_ref, low_ref, upp_ref, pk_ref, onek_ref,
                           qta_ref, ka_ref, vta_ref, kt_ref, vt_ref, lft_ref, gate_ref, ccarry, rcarry, *, per):
    @pl.when(pl.program_id(0) % per == 0)
    def _():
        ccarry[...] = jnp.zeros_like(ccarry)
        rcarry[...] = jnp.zeros_like(rcarry)

    tm = x_ref.shape[0]
    h = _rms_norm(x_ref[...], g_ref[...]).astype(BF16)
    lf = _log_sigmoid(_dot(h, wf_ref[...]) + bf_ref[...])
    lft = _log_sigmoid(_dot_nt(wft_ref[...], h) + bft_ref[...])
    lft_ref[0] = lft
    lp = _split3(lf)
    ccol = _dot(low_ref[...], lp[0]) + _dot(low_ref[...], lp[1]) + _dot(low_ref[...], lp[2]) + ccarry[...]
    crow = _dot3(_split3(lft), upp_ref[...]) + rcarry[...]
    ccarry[...] = ccol[tm - 1:tm, :]
    rcarry[...] = crow[:, tm - 1:tm]
    cp = _split3(ccol)
    ka = _dot(h, wk_ref[...]) + onek_ref[...]
    for n in range(N_PIECES):
        ka = ka + _dot(cp[n], pk_ref[n])
    ka_ref[...] = ka.astype(BF16)
    _store_q_aug(qta_ref, _dot_nt(wqt_ref[...], h), [p.astype(F32) for p in _split3(crow)])
    kt_ref[0] = _dot_nt(wkt_ref[...], h)
    vt = _dot_nt(wvt_ref[...], h)
    vt_ref[0] = vt
    _store_v_aug(vta_ref, vt)
    gate_ref[...] = _silu(_dot(h, wz_ref[...])).astype(BF16)


def _bias_placement():
    pk = np.zeros((N_PIECES, N_HEADS, K_AUG_LANES), np.float32)
    onek = np.zeros((1, K_AUG_LANES), np.float32)
    for head in range(N_HEADS):
        kvh, g = divmod(head, GROUP)
        slot = kvh * QK_AUG + HEAD_DIM + BIAS_SLOTS * g
        for n in range(N_PIECES):
            onek[0, slot + n] = 1.0
            pk[n, head, slot + N_PIECES + n] = -1.0
    return jnp.asarray(pk, BF16), jnp.asarray(onek)


def _attn_weights(wq, wk, wv):
    wk_pad = jnp.pad(wk.reshape(D_MODEL, N_KV, HEAD_DIM), ((0, 0), (0, 0), (0, QK_AUG - HEAD_DIM)))
    return dict(wqt=(wq * ATT_SCALE).T.astype(BF16), wk=wk_pad.reshape(D_MODEL, K_AUG_LANES).astype(BF16),
                wkt=wk.T.astype(BF16), wvt=wv.T.astype(BF16))


def _pre_fox_prompt(x, g, w, nb):
    rows = x.shape[0]
    seq = rows // nb
    tm = min(256, seq)
    per = seq // tm
    row, fmaj = _row_specs(rows, nb, tm)
    consts = [_tri(tm, False, False), _tri(tm, False, True), *_bias_placement()]
    names = ("wqt", "wk", "wkt", "wvt", "wf", "wft", "wz", "bf", "bft")
    return pl.pallas_call(
        functools.partial(_pre_fox_prompt_kernel, per=per),
        grid=(rows // tm,),
        in_specs=[row(D_MODEL), _full((1, D_MODEL))] + [_full(w[n].shape) for n in names]
                 + [_full(c.shape) for c in consts],
        out_specs=[fmaj(Q_AUG_ROWS), row(K_AUG_LANES), fmaj(N_KV * V_AUG), fmaj(KV_WIDTH), fmaj(KV_WIDTH),
                   fmaj(N_HEADS), row(D_MODEL)],
        out_shape=[jax.ShapeDtypeStruct((nb, Q_AUG_ROWS, seq), BF16), jax.ShapeDtypeStruct((rows, K_AUG_LANES), BF16),
                   jax.ShapeDtypeStruct((nb, N_KV * V_AUG, seq), BF16),
                   jax.ShapeDtypeStruct((nb, KV_WIDTH, seq), F32), jax.ShapeDtypeStruct((nb, KV_WIDTH, seq), F32),
                   jax.ShapeDtypeStruct((nb, N_HEADS, seq), F32), jax.ShapeDtypeStruct((rows, D_MODEL), BF16)],
        scratch_shapes=[pltpu.VMEM((1, N_HEADS), F32), pltpu.VMEM((N_HEADS, 1), F32)],
        compiler_params=_cparams("arbitrary"),
        name="pre_fox_prompt",
    )(x, g, *[w[n] for n in names], *consts)


def _pre_swa_prompt_kernel(x_ref, g_ref, wqt_ref, wk_ref, wkt_ref, wvt_ref, wz_ref,
                           cos_ref, sa_ref, sb_ref, cost_ref, sint_ref,
                           qta_ref, ka_ref, vta_ref, kt_ref, vt_ref, gate_ref):
    h = _rms_norm(x_ref[...], g_ref[...]).astype(BF16)
    cos_t, sin_t = cost_ref[...], sint_ref[...]
    _store_q_aug(qta_ref, _rope_fmaj(_dot_nt(wqt_ref[...], h), cos_t, sin_t, HEAD_DIM), None)
    ka_ref[...] = _rope_rows(_dot(h, wk_ref[...]), cos_ref[...], sa_ref[...], sb_ref[...]).astype(BF16)
    kt_ref[0] = _rope_fmaj(_dot_nt(wkt_ref[...], h), cos_t, sin_t, HEAD_DIM)
    vt = _dot_nt(wvt_ref[...], h)
    vt_ref[0] = vt
    _store_v_aug(vta_ref, vt)
    gate_ref[...] = _silu(_dot(h, wz_ref[...])).astype(BF16)


def _pre_swa_prompt(x, g, w, tables, nb):
    rows = x.shape[0]
    seq = rows // nb
    tm = min(256, seq)
    per = seq // tm
    row, fmaj = _row_specs(rows, nb, tm)
    tab = pl.BlockSpec((tm, LANES), lambda i: (i % per, 0))
    tab_t = pl.BlockSpec((ROT_HALF, tm), lambda i: (0, i % per))
    names = ("wqt", "wk", "wkt", "wvt", "wz")
    return pl.pallas_call(
        _pre_swa_prompt_kernel,
        grid=(rows // tm,),
        in_specs=[row(D_MODEL), _full((1, D_MODEL))] + [_full(w[n].shape) for n in names]
                 + [tab, tab, tab, tab_t, tab_t],
        out_specs=[fmaj(Q_AUG_ROWS), row(K_AUG_LANES), fmaj(N_KV * V_AUG), fmaj(KV_WIDTH), fmaj(KV_WIDTH),
                   row(D_MODEL)],
        out_shape=[jax.ShapeDtypeStruct((nb, Q_AUG_ROWS, seq), BF16), jax.ShapeDtypeStruct((rows, K_AUG_LANES), BF16),
                   jax.ShapeDtypeStruct((nb, N_KV * V_AUG, seq), BF16),
                   jax.ShapeDtypeStruct((nb, KV_WIDTH, seq), F32), jax.ShapeDtypeStruct((nb, KV_WIDTH, seq), F32),
                   jax.ShapeDtypeStruct((rows, D_MODEL), BF16)],
        compiler_params=_cparams("parallel"),
        name="pre_swa_prompt",
    )(x, g, *[w[n] for n in names], *tables)


def _pre_fox_decode_kernel(x_ref, g_ref, wq_ref, wkt_ref, wvt_ref, wft_ref, wz_ref, bft_ref,
                           q_ref, kt_ref, vt_ref, lft_ref, gate_ref):
    h = _rms_norm(x_ref[...], g_ref[...]).astype(BF16)
    q_ref[...] = _dot(h, wq_ref[...]).astype(BF16)
    kt_ref[0] = _dot_nt(wkt_ref[...], h)
    vt_ref[0] = _dot_nt(wvt_ref[...], h)
    lft_ref[0] = _log_sigmoid(_dot_nt(wft_ref[...], h) + bft_ref[...])
    gate_ref[...] = _silu(_dot(h, wz_ref[...])).astype(BF16)


def _pre_fox_decode(x, g, w):
    rows = x.shape[0]
    row, fmaj = _row_specs(rows, 1, rows)
    names = ("wq", "wkt", "wvt", "wft", "wz", "bft")
    return pl.pallas_call(
        _pre_fox_decode_kernel,
        grid=(1,),
        in_specs=[row(D_MODEL), _full((1, D_MODEL))] + [_full(w[n].shape) for n in names],
        out_specs=[row(D_MODEL), fmaj(KV_WIDTH), fmaj(KV_WIDTH), fmaj(N_HEADS), row(D_MODEL)],
        out_shape=[jax.ShapeDtypeStruct((rows, D_MODEL), BF16),
                   jax.ShapeDtypeStruct((1, KV_WIDTH, rows), F32), jax.ShapeDtypeStruct((1, KV_WIDTH, rows), F32),
                   jax.ShapeDtypeStruct((1, N_HEADS, rows), F32), jax.ShapeDtypeStruct((rows, D_MODEL), BF16)],
        compiler_params=_cparams("arbitrary"),
        name="pre_fox_decode",
    )(x, g, *[w[n] for n in names])


def _pre_swa_decode_kernel(x_ref, g_ref, wq_ref, wkt_ref, wvt_ref, wz_ref, cos_ref, sa_ref, sb_ref,
                           cost_ref, sint_ref, q_ref, kt_ref, vt_ref, gate_ref):
    h = _rms_norm(x_ref[...], g_ref[...]).astype(BF16)
    q_ref[...] = _rope_rows(_dot(h, wq_ref[...]), cos_ref[...], sa_ref[...], sb_ref[...]).astype(BF16)
    kt_ref[0] = _rope_fmaj(_dot_nt(wkt_ref[...], h), cost_ref[...], sint_ref[...], HEAD_DIM)
    vt_ref[0] = _dot_nt(wvt_ref[...], h)
    gate_ref[...] = _silu(_dot(h, wz_ref[...])).astype(BF16)


def _pre_swa_decode(x, g, w, tables):
    rows = x.shape[0]
    row, fmaj = _row_specs(rows, 1, rows)
    names = ("wq", "wkt", "wvt", "wz")
    return pl.pallas_call(
        _pre_swa_decode_kernel,
        grid=(1,),
        in_specs=[row(D_MODEL), _full((1, D_MODEL))] + [_full(w[n].shape) for n in names]
                 + [_full(t.shape) for t in tables],
        out_specs=[row(D_MODEL), fmaj(KV_WIDTH), fmaj(KV_WIDTH), row(D_MODEL)],
        out_shape=[jax.ShapeDtypeStruct((rows, D_MODEL), BF16),
                   jax.ShapeDtypeStruct((1, KV_WIDTH, rows), F32), jax.ShapeDtypeStruct((1, KV_WIDTH, rows), F32),
                   jax.ShapeDtypeStruct((rows, D_MODEL), BF16)],
        compiler_params=_cparams("arbitrary"),
        name="pre_swa_decode",
    )(x, g, *[w[n] for n in names], *tables)


def _pre_conv_kernel(x_ref, g_ref, wa_ref, wg_ref, wz_ref, glu_ref, gate_ref):
    h = _rms_norm(x_ref[...], g_ref[...]).astype(BF16)
    glu_ref[...] = _dot(h, wa_ref[...]) * _sigmoid(_dot(h, wg_ref[...]))
    gate_ref[...] = _silu(_dot(h, wz_ref[...])).astype(BF16)


def _pre_conv(x, g, w):
    rows = x.shape[0]
    tm = min(512, rows)
    row = pl.BlockSpec((tm, D_MODEL), lambda i: (i, 0))
    return pl.pallas_call(
        _pre_conv_kernel,
        grid=(rows // tm,),
        in_specs=[row, _full((1, D_MODEL)), _full(w["wa"].shape), _full(w["wg"].shape), _full(w["wz"].shape)],
        out_specs=[row, row],
        out_shape=[jax.ShapeDtypeStruct((rows, D_MODEL), F32), jax.ShapeDtypeStruct((rows, D_MODEL), BF16)],
        compiler_params=_cparams("parallel"),
        name="pre_conv",
    )(x, g, w["wa"], w["wg"], w["wz"])


def _post_mix_kernel(*refs, gated, final):
    refs = list(refs)
    a_ref = refs.pop(0)
    gate_ref = refs.pop(0) if gated else None
    x_ref, p_ref, wo_ref, ng_ref, wg_ref, bg_ref, wp_ref = refs[:7]
    fg_ref = refs[7] if final else None
    o_ref = refs[-1]
    a = a_ref[...]
    if gated:
        a = (a * gate_ref[...].astype(F32)).astype(BF16)
    x1 = x_ref[...] + _dot(a, wo_ref[...])
    h = _rms_norm(x1, ng_ref[...]).astype(BF16)
    gate2 = _sigmoid(_dot(h, wg_ref[...]) + bg_ref[...])
    x2 = x1 + gate2 * _dot(p_ref[...].astype(BF16), wp_ref[...])
    if final:
        x2 = _rms_norm(x2, fg_ref[...])
    o_ref[...] = x2


def _post_mix(a, gate, x, p, w, final_g):
    rows = x.shape[0]
    tm = min(512, rows)
    row = lambda width: pl.BlockSpec((tm, width), lambda i: (i, 0))
    gated, final = gate is not None, final_g is not None
    pe = p.shape[1]
    args = [a] + ([gate] if gated else []) + [x, p, w["wo"], w["ng"], w["wg"], w["bg"], w["wp"]]
    specs = [row(D_MODEL)] * (2 if gated else 1)
    specs += [row(D_MODEL), row(pe), _full((D_MODEL, D_MODEL)), _full((1, D_MODEL)), _full((D_MODEL, D_MODEL)),
              _full((1, D_MODEL)), _full((pe, D_MODEL))]
    if final:
        args.append(final_g)
        specs.append(_full((1, D_MODEL)))
    return pl.pallas_call(
        functools.partial(_post_mix_kernel, gated=gated, final=final),
        grid=(rows // tm,),
        in_specs=specs,
        out_specs=row(D_MODEL),
        out_shape=jax.ShapeDtypeStruct((rows, D_MODEL), F32),
        compiler_params=_cparams("parallel"),
        name="post_mix",
    )(*args)


def _softmax_step(st, m, acc, vta):
    m_new = jnp.maximum(m, jnp.max(st, axis=0, keepdims=True))
    alpha = jnp.exp(m - m_new)
    p = jnp.exp(st - m_new).astype(BF16)
    return m_new, alpha * acc + _dot(vta, p)


def _finish_heads(accs, denoms, gate_ref, o_ref):
    outs = [(acc[0:HEAD_DIM, :] / den).T for acc, den in zip(accs, denoms)]
    o_ref[...] = (jnp.concatenate(outs, axis=1) * gate_ref[...].astype(F32)).astype(BF16)


def _fox_prompt_kernel(qta_ref, ka_ref, vta_ref, gate_ref, o_ref, *, tq, tk):
    i = pl.program_id(2)
    full = (i * tq) // tk
    causal = (lax.broadcasted_iota(jnp.int32, (tk, tq), 0)
              <= lax.broadcasted_iota(jnp.int32, (tk, tq), 1) + (i * tq - full * tk))
    qts = [qta_ref[0, g * QK_AUG:(g + 1) * QK_AUG, :] for g in range(GROUP)]

    def step(j, carry, masked):
        off = pl.multiple_of(j * tk, tk)
        ka = ka_ref[pl.ds(off, tk), :]
        vta = vta_ref[0, :, pl.ds(off, tk)]
        sts = [_dot(ka, qts[g]) for g in range(GROUP)]
        if masked:
            sts = [jnp.where(causal, st, -jnp.inf) for st in sts]
        return tuple(_softmax_step(sts[g], carry[g][0], carry[g][1], vta) for g in range(GROUP))

    init = tuple((jnp.full((1, tq), -jnp.inf, F32), jnp.zeros((V_AUG, tq), F32)) for _ in range(GROUP))
    carry = lax.fori_loop(0, full, functools.partial(step, masked=False), init)
    carry = step(full, carry, True)
    accs = [c[1] for c in carry]
    _finish_heads(accs, [acc[HEAD_DIM:HEAD_DIM + 1, :] for acc in accs], gate_ref, o_ref)


def _fox_prompt(qta, ka, vta, gate):
    batch, _, seq = qta.shape
    rows = batch * seq
    tq = min(512, seq)
    tk = min(512, seq)
    assert seq % tk == 0 and tk % tq == 0
    nq = seq // tq
    width = GROUP * HEAD_DIM
    qspec = pl.BlockSpec((tq, width), lambda b, h, i: (b * nq + i, h))
    return pl.pallas_call(
        functools.partial(_fox_prompt_kernel, tq=tq, tk=tk),
        grid=(batch, N_KV, nq),
        in_specs=[pl.BlockSpec((1, GROUP * QK_AUG, tq), lambda b, h, i: (b, h, i)),
                  pl.BlockSpec((seq, QK_AUG), lambda b, h, i: (b, h)),
                  pl.BlockSpec((1, V_AUG, seq), lambda b, h, i: (b, h, 0)),
                  qspec],
        out_specs=qspec,
        out_shape=jax.ShapeDtypeStruct((rows, D_MODEL), BF16),
        compiler_params=_cparams("parallel", "parallel", "arbitrary"),
        name="fox_prompt",
    )(qta, ka, vta, gate)


SWA_QB = 4


def _swa_prompt_kernel(qta_ref, ka_ref, vta_ref, sink_ref, gate_ref, o_ref):
    n0 = pl.program_id(2) * SWA_QB
    ki = lax.broadcasted_iota(jnp.int32, (2 * Q_BLOCK, Q_BLOCK), 0)
    qi = lax.broadcasted_iota(jnp.int32, (2 * Q_BLOCK, Q_BLOCK), 1)
    starts, sts = [], []
    for r in range(SWA_QB):
        n = n0 + r
        start = pl.multiple_of(jnp.maximum(n - 1, 0) * Q_BLOCK, Q_BLOCK)
        rel = (n * Q_BLOCK + qi) - (start + ki)
        allowed = (rel >= 0) & (rel < WINDOW)
        ka = ka_ref[pl.ds(start, 2 * Q_BLOCK), :]
        starts.append(start)
        sts.append([jnp.where(allowed, _dot(ka, qta_ref[0, g * QK_AUG:(g + 1) * QK_AUG,
                                                        r * Q_BLOCK:(r + 1) * Q_BLOCK]), -jnp.inf)
                    for g in range(GROUP)])
    sinks = [sink_ref[0, :, g:g + 1] for g in range(GROUP)]
    ms = [[jnp.maximum(jnp.max(sts[r][g], axis=0, keepdims=True), sinks[g]) for g in range(GROUP)]
          for r in range(SWA_QB)]
    ps = [[jnp.exp(sts[r][g] - ms[r][g]).astype(BF16) for g in range(GROUP)] for r in range(SWA_QB)]
    for r in range(SWA_QB):
        vta = vta_ref[0, :, pl.ds(starts[r], 2 * Q_BLOCK)]
        accs = [_dot(vta, ps[r][g]) for g in range(GROUP)]
        denoms = [accs[g][HEAD_DIM:HEAD_DIM + 1, :] + jnp.exp(sinks[g] - ms[r][g]) for g in range(GROUP)]
        rows = slice(r * Q_BLOCK, (r + 1) * Q_BLOCK)
        _finish_heads(accs, denoms, gate_ref.at[rows, :], o_ref.at[rows, :])


def _swa_prompt(qta, ka, vta, sinks, gate):
    batch, _, seq = qta.shape
    rows = batch * seq
    tq = SWA_QB * Q_BLOCK
    nq = seq // tq
    width = GROUP * HEAD_DIM
    qspec = pl.BlockSpec((tq, width), lambda b, h, i: (b * nq + i, h))
    return pl.pallas_call(
        _swa_prompt_kernel,
        grid=(batch, N_KV, nq),
        in_specs=[pl.BlockSpec((1, GROUP * QK_AUG, tq), lambda b, h, i: (b, h, i)),
                  pl.BlockSpec((seq, QK_AUG), lambda b, h, i: (b, h)),
                  pl.BlockSpec((1, V_AUG, seq), lambda b, h, i: (b, h, 0)),
                  pl.BlockSpec((1, 1, GROUP), lambda b, h, i: (h, 0, 0)),
                  qspec],
        out_specs=qspec,
        out_shape=jax.ShapeDtypeStruct((rows, D_MODEL), BF16),
        compiler_params=_cparams("parallel", "parallel", "arbitrary"),
        name="swa_prompt",
    )(qta, ka, vta, sinks.reshape(N_KV, 1, GROUP), gate)


def _block_diag_mask():
    r = lax.broadcasted_iota(jnp.int32, (DEC_ROWS, KV_WIDTH), 0)
    c = lax.broadcasted_iota(jnp.int32, (DEC_ROWS, KV_WIDTH), 1)
    return ((r % N_HEADS) // GROUP) == (c // HEAD_DIM)


def _fold_heads(o_full):
    om = jnp.where(_block_diag_mask(), o_full, 0.0)
    return (om[:, 0:HEAD_DIM] + om[:, HEAD_DIM:2 * HEAD_DIM]
            + om[:, 2 * HEAD_DIM:3 * HEAD_DIM] + om[:, 3 * HEAD_DIM:4 * HEAD_DIM])


def _new_token_mask():
    lane = lax.broadcasted_iota(jnp.int32, (DEC_ROWS, PAGE), 1)
    t = lax.broadcasted_iota(jnp.int32, (DEC_ROWS, PAGE), 0) // N_HEADS
    return (lane >= NEW_LANE) & (lane - NEW_LANE <= t)


def _fox_decode_kernel(pt_ref, qbd_ref, knew_ref, vnew_ref, lfnew_ref, kc_ref, vc_ref, lc_ref, o_ref,
                       kbuf, vbuf, lbuf, sems, *, layer, cp):
    b = pl.program_id(0)
    nb = pl.num_programs(0)
    nch = pt_ref.shape[1] // cp

    def copies(seq, ci, slot):
        base = (nch - 1 - ci) * cp
        out = []
        for p in range(cp):
            page = pt_ref[seq, base + p]
            out.append(pltpu.make_async_copy(kc_ref.at[layer, page], kbuf.at[slot, p], sems.at[0, slot]))
            out.append(pltpu.make_async_copy(vc_ref.at[layer, page], vbuf.at[slot, p], sems.at[1, slot]))
            out.append(pltpu.make_async_copy(lc_ref.at[layer, page], lbuf.at[slot, p], sems.at[2, slot]))
        return out

    @pl.when(b == 0)
    def _():
        for c in copies(0, 0, 0):
            c.start()

    qbd = qbd_ref[0]

    x = lfnew_ref[0]
    cn = x + pltpu.roll(x, 1, 1) + pltpu.roll(x, 2, 1) + pltpu.roll(x, 3, 1)
    s_new = _dot(qbd, knew_ref[0].astype(BF16))
    cqs = [cn[:, NEW_LANE + t:NEW_LANE + t + 1] for t in range(DEC_T)]
    s_new = s_new + jnp.concatenate([cqs[t] - cn for t in range(DEC_T)], axis=0)
    s_new = jnp.where(_new_token_mask(), s_new, -jnp.inf)
    cq = jnp.concatenate(cqs, axis=0)
    m = jnp.max(s_new, axis=-1, keepdims=True)
    p = jnp.exp(s_new - m)
    l = jnp.sum(p, axis=-1, keepdims=True)
    acc = _dot_nt(p.astype(BF16), vnew_ref[0].astype(BF16))

    sfx = jnp.concatenate([_tri(PAGE, strict=True, upper=False), jnp.ones((PAGE, LANES), BF16)], axis=1)

    def chunk(ci, carry):
        m, l, acc, later = carry
        slot = (b * nch + ci) % 2

        @pl.when(ci + 1 < nch)
        def _():
            for c in copies(b, ci + 1, 1 - slot):
                c.start()

        @pl.when(jnp.logical_and(ci + 1 == nch, b + 1 < nb))
        def _():
            for c in copies(b + 1, 0, 1 - slot):
                c.start()

        for c in copies(b, ci, slot):
            c.wait()

        st = _dot3(_split3(lbuf[slot].reshape(cp * N_HEADS, PAGE)), sfx).reshape(cp, N_HEADS, 2 * LANES)
        bias = [None] * cp
        for pg in reversed(range(cp)):
            bias[pg] = jnp.concatenate([st[pg, :, :LANES] + later] * DEC_T, axis=0) + cq
            later = later + st[pg, :, LANES:]
        ss = [_dot(qbd, kbuf[slot, pg].astype(BF16)) + bias[pg] for pg in range(cp)]
        per = cp // DEC_GROUPS if cp % DEC_GROUPS == 0 else cp
        parts = []
        for lo in range(0, cp, per):
            grp = ss[lo:lo + per]
            m_loc = jnp.max(functools.reduce(jnp.maximum, grp), axis=-1, keepdims=True)
            ps = [jnp.exp(s - m_loc) for s in grp]
            l_loc = jnp.sum(functools.reduce(jnp.add, ps), axis=-1, keepdims=True)
            parts.append((m_loc, l_loc, lo, ps))
        accs = []
        for m_loc, l_loc, lo, ps in parts:
            a = _dot_nt(ps[0].astype(BF16), vbuf[slot, lo].astype(BF16))
            for n in range(1, len(ps)):
                a = a + _dot_nt(ps[n].astype(BF16), vbuf[slot, lo + n].astype(BF16))
            accs.append(a)
        m_new = functools.reduce(jnp.maximum, [m] + [pt[0] for pt in parts])
        alpha = jnp.exp(m - m_new)
        l, acc = alpha * l, alpha * acc
        for (m_loc, l_loc, _, _), a in zip(parts, accs):
            w = jnp.exp(m_loc - m_new)
            l, acc = l + w * l_loc, acc + w * a
        return m_new, l, acc, later

    m, l, acc, _ = lax.fori_loop(0, nch, chunk, (m, l, acc, jnp.zeros((N_HEADS, LANES), F32)))
    o_ref[0] = _fold_heads(acc / l)


def _fox_decode(page_table, qbd, knew, vnew, lfnew, cache_kt, cache_vt, cache_lft, layer):
    batch, n_pages = page_table.shape
    cp = 16
    while n_pages % cp:
        cp //= 2
    seq3 = lambda r, c: pl.BlockSpec((1, r, c), lambda b, pt: (b, 0, 0))
    grid_spec = pltpu.PrefetchScalarGridSpec(
        num_scalar_prefetch=1,
        grid=(batch,),
        in_specs=[seq3(DEC_ROWS, KV_WIDTH), seq3(KV_WIDTH, PAGE), seq3(KV_WIDTH, PAGE), seq3(N_HEADS, PAGE),
                  pl.BlockSpec(memory_space=pl.ANY), pl.BlockSpec(memory_space=pl.ANY),
                  pl.BlockSpec(memory_space=pl.ANY)],
        out_specs=seq3(DEC_ROWS, HEAD_DIM),
        scratch_shapes=[pltpu.VMEM((2, cp, KV_WIDTH, PAGE), F32), pltpu.VMEM((2, cp, KV_WIDTH, PAGE), F32),
                        pltpu.VMEM((2, cp, N_HEADS, PAGE), F32), pltpu.SemaphoreType.DMA((3, 2))],
    )
    return pl.pallas_call(
        functools.partial(_fox_decode_kernel, layer=layer, cp=cp),
        grid_spec=grid_spec,
        out_shape=jax.ShapeDtypeStruct((batch, DEC_ROWS, HEAD_DIM), F32),
        compiler_params=_cparams("arbitrary"),
        name="fox_decode",
    )(page_table, qbd, knew, vnew, lfnew, cache_kt, cache_vt, cache_lft)


def _swa_decode_kernel(qbd_ref, knew_ref, vnew_ref, sk_ref, sv_ref, sink_ref, o_ref, nk_ref, nv_ref):
    qbd = qbd_ref[0]
    sk, sv, kn, vn = sk_ref[0, 0], sv_ref[0, 0], knew_ref[0], vnew_ref[0]
    lane = lax.broadcasted_iota(jnp.int32, (DEC_ROWS, PAGE), 1)
    t = lax.broadcasted_iota(jnp.int32, (DEC_ROWS, PAGE), 0) // N_HEADS
    s_old = jnp.where(lane > t, _dot(qbd, sk.astype(BF16)), -jnp.inf)
    s_new = jnp.where(_new_token_mask(), _dot(qbd, kn.astype(BF16)), -jnp.inf)
    sink = sink_ref[...]
    m = jnp.maximum(jnp.maximum(jnp.max(s_old, axis=-1, keepdims=True), jnp.max(s_new, axis=-1, keepdims=True)),
                    sink)
    p_old = jnp.exp(s_old - m)
    p_new = jnp.exp(s_new - m)
    denom = (jnp.sum(p_old, axis=-1, keepdims=True) + jnp.sum(p_new, axis=-1, keepdims=True)
             + jnp.exp(sink - m))
    o_full = _dot_nt(p_old.astype(BF16), sv.astype(BF16)) + _dot_nt(p_new.astype(BF16), vn.astype(BF16))
    o_ref[0] = _fold_heads(o_full / denom)
    keep_new = lax.broadcasted_iota(jnp.int32, (KV_WIDTH, PAGE), 1) >= NEW_LANE
    nk_ref[0] = jnp.where(keep_new, kn, pltpu.roll(sk, NEW_LANE, 1))
    nv_ref[0] = jnp.where(keep_new, vn, pltpu.roll(sv, NEW_LANE, 1))


def _swa_decode(qbd, knew, vnew, state_kt, state_vt, sink_rows, layer):
    batch, buf = state_kt.shape[1], state_kt.shape[3]
    assert buf == WINDOW == PAGE
    seq3 = lambda r, c: pl.BlockSpec((1, r, c), lambda b: (b, 0, 0))
    st = pl.BlockSpec((1, 1, KV_WIDTH, buf), lambda b: (layer, b, 0, 0))
    return pl.pallas_call(
        _swa_decode_kernel,
        grid=(batch,),
        in_specs=[seq3(DEC_ROWS, KV_WIDTH), seq3(KV_WIDTH, PAGE), seq3(KV_WIDTH, PAGE), st, st,
                  _full((DEC_ROWS, 1))],
        out_specs=[seq3(DEC_ROWS, HEAD_DIM), seq3(KV_WIDTH, buf), seq3(KV_WIDTH, buf)],
        out_shape=[jax.ShapeDtypeStruct((batch, DEC_ROWS, HEAD_DIM), F32),
                   jax.ShapeDtypeStruct((batch, KV_WIDTH, buf), F32),
                   jax.ShapeDtypeStruct((batch, KV_WIDTH, buf), F32)],
        compiler_params=_cparams("parallel"),
        name="swa_decode",
    )(qbd, knew, vnew, state_kt, state_vt, sink_rows)


HALO = 32
CONV_CHUNK = 32


def _conv_tail(y, gate, lng, lnb):
    yc = y - jnp.mean(y, axis=-1, keepdims=True)
    yn = yc * lax.rsqrt(jnp.mean(yc * yc, axis=-1, keepdims=True) + NORM_EPS) * lng + lnb
    return (_silu(yn) * gate.astype(F32)).astype(BF16)


def _conv_prompt_kernel(x_ref, g_ref, wa_ref, wg_ref, wz_ref, dw_ref, db_ref, lng_ref, lnb_ref,
                        a_ref, st_ref, buf, shifted, gbuf):
    i = pl.program_id(1)
    tm = x_ref.shape[0]
    h = _rms_norm(x_ref[...], g_ref[...]).astype(BF16)
    glu = _dot(h, wa_ref[...]) * _sigmoid(_dot(h, wg_ref[...]))
    gate = _silu(_dot(h, wz_ref[...]))

    @pl.when(i == 0)
    def _():
        buf[0:HALO, :] = jnp.zeros((HALO, D_MODEL), F32)

    buf[HALO:HALO + tm, :] = glu
    gbuf[...] = gate
    first = HALO - CONV_HIST
    for r in range(SUBLANES):
        span = tm + SUBLANES * ((CONV_HIST - r) // SUBLANES)
        shifted[r, 0:span, :] = buf[first + r:first + r + span, :]

    def chunk(c, _):
        off = pl.multiple_of(c * CONV_CHUNK, CONV_CHUNK)
        y = jnp.zeros((CONV_CHUNK, D_MODEL), F32) + db_ref[...]
        for w in range(CONV_WIDTH):
            start = off + SUBLANES * (w // SUBLANES)
            y = y + shifted[w % SUBLANES, pl.ds(start, CONV_CHUNK), :] * dw_ref[w:w + 1, :]
        rows = pl.ds(off, CONV_CHUNK)
        a_ref[rows, :] = _conv_tail(y, gbuf[rows, :], lng_ref[...], lnb_ref[...])
        return 0

    lax.fori_loop(0, tm // CONV_CHUNK, chunk, 0)

    @pl.when(i == pl.num_programs(1) - 1)
    def _():
        st_ref[0] = buf[HALO + tm - CONV_HIST:HALO + tm, :]

    buf[0:HALO, :] = buf[tm:tm + HALO, :]


def _conv_prompt(x, g, w, batch, seq):
    rows = batch * seq
    tm = min(256, seq)
    nt = seq // tm
    row = pl.BlockSpec((tm, D_MODEL), lambda b, i: (b * nt + i, 0))
    return pl.pallas_call(
        _conv_prompt_kernel,
        grid=(batch, nt),
        in_specs=[row, _full((1, D_MODEL)), _full(w["wa"].shape), _full(w["wg"].shape), _full(w["wz"].shape),
                  _full((CONV_WIDTH, D_MODEL)), _full((1, D_MODEL)), _full((1, D_MODEL)), _full((1, D_MODEL))],
        out_specs=[row, pl.BlockSpec((1, CONV_HIST, D_MODEL), lambda b, i: (b, 0, 0))],
        out_shape=[jax.ShapeDtypeStruct((rows, D_MODEL), BF16),
                   jax.ShapeDtypeStruct((batch, CONV_HIST, D_MODEL), F32)],
        scratch_shapes=[pltpu.VMEM((HALO + tm, D_MODEL), F32),
                        pltpu.VMEM((SUBLANES, tm + SUBLANES * (CONV_HIST // SUBLANES), D_MODEL), F32),
                        pltpu.VMEM((tm, D_MODEL), F32)],
        compiler_params=_cparams("arbitrary", "arbitrary"),
        name="conv_prompt",
    )(x, g, w["wa"], w["wg"], w["wz"], w["dw"], w["db"], w["lng"], w["lnb"])


def _conv_decode_kernel(glu_ref, gate_ref, st_ref, dw_ref, db_ref, lng_ref, lnb_ref, a_ref, ns_ref, pad):
    sb = glu_ref.shape[1]
    pad[0:CONV_HIST] = st_ref[0]
    pad[CONV_HIST:CONV_HIST + DEC_T] = glu_ref[...]
    y = jnp.zeros((DEC_T, sb, D_MODEL), F32) + db_ref[...]
    for w in range(CONV_WIDTH):
        y = y + pad[w:w + DEC_T] * dw_ref[w:w + 1, :]
    a_ref[...] = _conv_tail(y, gate_ref[...], lng_ref[...], lnb_ref[...])
    ns_ref[...] = pad[DEC_T:DEC_T + CONV_HIST]


def _conv_decode(glu, gate, state, w, layer):
    batch = state.shape[2]
    sb = 16
    while batch % sb:
        sb //= 2
    tok = pl.BlockSpec((DEC_T, sb, D_MODEL), lambda i: (0, i, 0))
    return pl.pallas_call(
        _conv_decode_kernel,
        grid=(batch // sb,),
        in_specs=[tok, tok, pl.BlockSpec((1, CONV_HIST, sb, D_MODEL), lambda i: (layer, 0, i, 0)),
                  _full((CONV_WIDTH, D_MODEL)), _full((1, D_MODEL)), _full((1, D_MODEL)), _full((1, D_MODEL))],
        out_specs=[tok, pl.BlockSpec((CONV_HIST, sb, D_MODEL), lambda i: (0, i, 0))],
        out_shape=[jax.ShapeDtypeStruct((DEC_T, batch, D_MODEL), BF16),
                   jax.ShapeDtypeStruct((CONV_HIST, batch, D_MODEL), F32)],
        scratch_shapes=[pltpu.VMEM((CONV_HIST + DEC_T, sb, D_MODEL), F32)],
        compiler_params=_cparams("parallel"),
        name="conv_decode",
    )(glu.reshape(DEC_T, batch, D_MODEL), gate.reshape(DEC_T, batch, D_MODEL), state,
      w["dw"], w["db"], w["lng"], w["lnb"])


def _rope_tables(pos):
    inv_freq = ROPE_THETA ** (-jnp.arange(ROT_HALF, dtype=F32) / ROT_HALF)
    ang = pos.astype(F32)[:, None] * inv_freq[None, :]
    cos, sin = jnp.cos(ang), jnp.sin(ang)
    n = pos.shape[0]
    rest = jnp.zeros((n, HEAD_DIM - ROT_DIM), F32)
    zero = jnp.zeros((n, ROT_HALF), F32)
    cos_h = jnp.concatenate([cos, cos, rest + 1.0], axis=1)
    sa_h = jnp.concatenate([-sin, zero, rest], axis=1)
    sb_h = jnp.concatenate([zero, sin, rest], axis=1)
    rep = LANES // HEAD_DIM
    return tuple(jnp.tile(t, (1, rep)) for t in (cos_h, sa_h, sb_h)) + (cos.T, sin.T)


def _decode_queries(q_rows, batch):
    q4 = q_rows.reshape(DEC_T, batch, N_HEADS, 1, HEAD_DIM).transpose(1, 0, 2, 3, 4)
    own = (jnp.arange(N_HEADS)[:, None] // GROUP) == jnp.arange(N_KV)[None, :]
    return jnp.where(own[None, None, :, :, None], q4, jnp.zeros((), q_rows.dtype)).reshape(batch, DEC_ROWS, KV_WIDTH)


def _new_token_pages(xt, batch):
    width = xt.shape[1]
    per_seq = xt.reshape(width, DEC_T, batch).transpose(2, 0, 1)
    return jnp.pad(per_seq, ((0, 0), (0, 0), (NEW_LANE, 0)))


def _decode_out_rows(o, batch):
    return o.reshape(batch, DEC_T, D_MODEL).transpose(1, 0, 2).reshape(DEC_T * batch, D_MODEL)


def _sample_major(xt, batch, inner):
    return xt.reshape(inner + (DEC_T, batch)).transpose((len(inner) + 1, len(inner)) + tuple(range(len(inner))))


def kernel(x_prompt, x_sample, cache_fox_k, cache_fox_v, cache_fox_lf, page_table, state_swa_k, state_swa_v,
           state_conv, p_prompt, p_sample, norm_g, fox_w_in, fox_b_f, fox_w_out, swa_w_in, swa_sinks, swa_w_out,
           conv_w_in, conv_dw_w, conv_dw_b, conv_ln_g, conv_ln_b, conv_w_out, pe_w_proj, pe_norm_g, pe_w_gate,
           pe_b_gate, final_norm_g):
    batch, seq, _ = x_prompt.shape
    dec_batch, dec_t, _ = x_sample.shape
    assert dec_t == DEC_T and seq % (SWA_QB * Q_BLOCK) == 0
    depth = norm_g.shape[0]
    past_len = page_table.shape[1] * PAGE
    buf = state_swa_k.shape[2]
    rows_p, rows_s = batch * seq, dec_batch * dec_t
    att = N_HEADS * HEAD_DIM

    xp = x_prompt.reshape(rows_p, D_MODEL)
    xs = x_sample.transpose(1, 0, 2).reshape(rows_s, D_MODEL)
    fmaj = lambda c: c.transpose(0, 1, 3, 4, 2).reshape(c.shape[:2] + (KV_WIDTH, c.shape[2]))
    cache_kt, cache_vt = fmaj(cache_fox_k), fmaj(cache_fox_v)
    cache_lft = cache_fox_lf.transpose(0, 1, 3, 2)
    st_kt, st_vt = fmaj(state_swa_k), fmaj(state_swa_v)
    st_conv = state_conv.transpose(0, 2, 1, 3)
    heads_major = lambda t: t.reshape(t.shape[0], N_KV, HEAD_DIM, t.shape[2]).transpose(0, 3, 1, 2)

    fox_k_p, fox_v_p, fox_lf_p, fox_k_s, fox_v_s, fox_lf_s = [], [], [], [], [], []
    swa_k_p, swa_v_p, swa_k_s, swa_v_s = [], [], [], []
    conv_p, conv_s = [], []

    for i in range(depth):
        kind, j = i % N_MIXERS, i // N_MIXERS
        g = norm_g[i].reshape(1, D_MODEL)
        if kind == 0:
            w_in = fox_w_in[j]
            wq, wk = w_in[:, :att], w_in[:, att:att + KV_WIDTH]
            wv = w_in[:, att + KV_WIDTH:att + 2 * KV_WIDTH]
            wf = w_in[:, att + 2 * KV_WIDTH:att + 2 * KV_WIDTH + N_HEADS]
            w = dict(_attn_weights(wq, wk, wv), wq=(wq * ATT_SCALE).astype(BF16),
                     wf=wf.astype(BF16), wft=wf.T.astype(BF16),
                     wz=w_in[:, att + 2 * KV_WIDTH + N_HEADS:].astype(BF16),
                     bf=fox_b_f[j].reshape(1, N_HEADS), bft=fox_b_f[j].reshape(N_HEADS, 1))
            w_out = fox_w_out[j]
            qta, ka, vta, kt, vt, lft, gate = _pre_fox_prompt(xp, g, w, batch)
            ap = _fox_prompt(qta, ka, vta, gate)
            gate_p = None
            fox_k_p.append(heads_major(kt))
            fox_v_p.append(heads_major(vt))
            fox_lf_p.append(lft.transpose(0, 2, 1))
            q, kt, vt, lft, gate_s = _pre_fox_decode(xs, g, w)
            o = _fox_decode(page_table, _decode_queries(q, dec_batch), _new_token_pages(kt, dec_batch),
                            _new_token_pages(vt, dec_batch), _new_token_pages(lft, dec_batch),
                            cache_kt, cache_vt, cache_lft, j)
            a_s = _decode_out_rows(o, dec_batch)
            fox_k_s.append(_sample_major(kt, dec_batch, (N_KV, HEAD_DIM)))
            fox_v_s.append(_sample_major(vt, dec_batch, (N_KV, HEAD_DIM)))
            fox_lf_s.append(_sample_major(lft, dec_batch, (N_HEADS,)))
        elif kind == 1:
            w_in = swa_w_in[j]
            wq, wk = w_in[:, :att], w_in[:, att:att + KV_WIDTH]
            wv = w_in[:, att + KV_WIDTH:att + 2 * KV_WIDTH]
            w = dict(_attn_weights(wq, wk, wv), wq=(wq * ATT_SCALE).astype(BF16),
                     wz=w_in[:, att + 2 * KV_WIDTH:].astype(BF16))
            w_out = swa_w_out[j]
            tables = _rope_tables(jnp.arange(seq, dtype=jnp.int32))
            qta, ka, vta, kt, vt, gate = _pre_swa_prompt(xp, g, w, tables, batch)
            ap = _swa_prompt(qta, ka, vta, swa_sinks[j], gate)
            gate_p = None
            swa_k_p.append(heads_major(kt[:, :, seq - buf:]))
            swa_v_p.append(heads_major(vt[:, :, seq - buf:]))
            pos_s = past_len + jnp.repeat(jnp.arange(DEC_T, dtype=jnp.int32), dec_batch)
            q, kt, vt, gate_s = _pre_swa_decode(xs, g, w, _rope_tables(pos_s))
            sink_rows = jnp.tile(swa_sinks[j].astype(F32), DEC_T).reshape(DEC_ROWS, 1)
            o, nkt, nvt = _swa_decode(_decode_queries(q, dec_batch), _new_token_pages(kt, dec_batch),
                                      _new_token_pages(vt, dec_batch), st_kt, st_vt, sink_rows, j)
            a_s = _decode_out_rows(o, dec_batch)
            swa_k_s.append(heads_major(nkt))
            swa_v_s.append(heads_major(nvt))
        else:
            w_in = conv_w_in[j]
            cd = w_in.shape[1] // 3
            w = dict(wa=w_in[:, :cd].astype(BF16), wg=w_in[:, cd:2 * cd].astype(BF16),
                     wz=w_in[:, 2 * cd:].astype(BF16), dw=conv_dw_w[j], db=conv_dw_b[j].reshape(1, cd),
                     lng=conv_ln_g[j].reshape(1, cd), lnb=conv_ln_b[j].reshape(1, cd))
            w_out = conv_w_out[j]
            ap, st = _conv_prompt(xp, g, w, batch, seq)
            gate_p = None
            conv_p.append(st)
            glu, gate = _pre_conv(xs, g, w)
            a_s, ns = _conv_decode(glu, gate, st_conv, w, j)
            a_s = a_s.reshape(rows_s, D_MODEL)
            gate_s = None
            conv_s.append(ns.transpose(1, 0, 2))
        wpost = dict(wo=w_out.astype(BF16), ng=pe_norm_g[i].reshape(1, D_MODEL), wg=pe_w_gate[i].astype(BF16),
                     bg=pe_b_gate[i].reshape(1, D_MODEL), wp=pe_w_proj[i].astype(BF16))
        fg = final_norm_g.reshape(1, D_MODEL) if i == depth - 1 else None
        xp = _post_mix(ap, gate_p, xp, p_prompt[i].reshape(rows_p, -1), wpost, fg)
        xs = _post_mix(a_s, gate_s, xs, p_sample[i].transpose(1, 0, 2).reshape(rows_s, -1), wpost, fg)

    return (xp.reshape(batch, seq, D_MODEL), xs.reshape(DEC_T, dec_batch, D_MODEL).transpose(1, 0, 2),
            jnp.stack(fox_k_p), jnp.stack(fox_v_p), jnp.stack(fox_lf_p),
            jnp.stack(fox_k_s), jnp.stack(fox_v_s), jnp.stack(fox_lf_s),
            jnp.stack(swa_k_p), jnp.stack(swa_v_p), jnp.stack(swa_k_s), jnp.stack(swa_v_s),
            jnp.stack(conv_p), jnp.stack(conv_s))
```

```python
import functools

import numpy as np

import jax
import jax.numpy as jnp
from jax import lax
from jax.experimental import pallas as pl
from jax.experimental.pallas import tpu as pltpu

F32 = jnp.float32
BF16 = jnp.bfloat16

D_MODEL = 1024
HEAD_DIM = 64
N_HEADS = 16
N_KV = 4
GROUP = 4
KV_WIDTH = N_KV * HEAD_DIM
ATT_SCALE = HEAD_DIM ** -0.5
N_MIXERS = 3
Q_BLOCK = 128
WINDOW = 128
ROPE_THETA = 500000.0
ROT_DIM = HEAD_DIM // 4
ROT_HALF = ROT_DIM // 2
CONV_WIDTH = 31
CONV_HIST = CONV_WIDTH - 1
NORM_EPS = 1e-6
PAGE = 128
LANES = 128
SUBLANES = 8
VMEM_LIMIT = 52 * 1024 * 1024
DEC_T = 4
DEC_ROWS = DEC_T * N_HEADS
NEW_LANE = PAGE - DEC_T
DEC_GROUPS = 4

QK_AUG = 128
BIAS_SLOTS = 8
N_PIECES = 3
V_AUG = 80
Q_AUG_ROWS = N_HEADS * QK_AUG
K_AUG_LANES = N_KV * QK_AUG

NT_DIMS = (((1,), (1,)), ((), ()))


def _cparams(*sem, flags=None):
    return pltpu.CompilerParams(dimension_semantics=sem, vmem_limit_bytes=VMEM_LIMIT, flags=flags)


def _dot(a, b):
    return jnp.dot(a, b, preferred_element_type=F32)


def _dot_nt(a, b):
    return lax.dot_general(a, b, NT_DIMS, preferred_element_type=F32)


def _sigmoid(x):
    return 1.0 / (1.0 + jnp.exp(-x))


def _silu(x):
    return x * _sigmoid(x)


def _log_sigmoid(x):
    return jnp.minimum(x, 0.0) - jnp.log1p(jnp.exp(-jnp.abs(x)))


def _rms_norm(x, g):
    return x * lax.rsqrt(jnp.mean(x * x, axis=-1, keepdims=True) + NORM_EPS) * g


def _split3(x):
    p1 = x.astype(BF16)
    r1 = x - p1.astype(F32)
    p2 = r1.astype(BF16)
    r2 = r1 - p2.astype(F32)
    return p1, p2, r2.astype(BF16)


def _dot3(pieces, m):
    return _dot(pieces[0], m) + _dot(pieces[1], m) + _dot(pieces[2], m)


def _full(shape):
    n = len(shape)
    return pl.BlockSpec(shape, lambda *_: (0,) * n)


def _tri(n, strict, upper):
    r = lax.broadcasted_iota(jnp.int32, (n, n), 0)
    c = lax.broadcasted_iota(jnp.int32, (n, n), 1)
    if upper:
        keep = (r < c) if strict else (r <= c)
    else:
        keep = (r > c) if strict else (r >= c)
    return jnp.where(keep, 1.0, 0.0).astype(BF16)


def _row_specs(rows, nb, tm):
    per = rows // nb // tm
    row = lambda width: pl.BlockSpec((tm, width), lambda i: (i, 0))
    fmaj = lambda width: pl.BlockSpec((1, width, tm), lambda i: (i // per, 0, i % per))
    return row, fmaj


def _rope_rows(u, cos, sa, sb):
    outs = []
    for c in range(u.shape[1] // LANES):
        blk = u[:, c * LANES:(c + 1) * LANES]
        outs.append(blk * cos + pltpu.roll(blk, LANES - ROT_HALF, 1) * sa + pltpu.roll(blk, ROT_HALF, 1) * sb)
    return jnp.concatenate(outs, axis=1)


def _rope_fmaj(ut, cos_t, sin_t, stride):
    assert ROT_HALF == SUBLANES
    outs = []
    for base in range(0, ut.shape[0], stride):
        x1 = ut[base:base + ROT_HALF, :]
        x2 = ut[base + ROT_HALF:base + ROT_DIM, :]
        outs += [x1 * cos_t - x2 * sin_t, x2 * cos_t + x1 * sin_t, ut[base + ROT_DIM:base + stride, :]]
    return jnp.concatenate(outs, axis=0)


def _store_q_aug(qta_ref, qt, bias_pieces):
    tm = qt.shape[1]
    sub = lax.broadcasted_iota(jnp.int32, (BIAS_SLOTS, tm), 0)
    ones = jnp.where((sub >= N_PIECES) & (sub < 2 * N_PIECES), 1.0, 0.0)
    for head in range(N_HEADS):
        parts = [qt[head * HEAD_DIM:(head + 1) * HEAD_DIM, :]]
        if bias_pieces is None:
            parts.append(jnp.zeros((QK_AUG - HEAD_DIM, tm), F32))
        else:
            g = head % GROUP
            grp = ones
            for n in reversed(range(N_PIECES)):
                grp = jnp.where(sub == n, bias_pieces[n][head:head + 1, :], grp)
            before, after = BIAS_SLOTS * g, QK_AUG - HEAD_DIM - BIAS_SLOTS * (g + 1)
            parts += ([jnp.zeros((before, tm), F32)] if before else []) + [grp]
            parts += [jnp.zeros((after, tm), F32)] if after else []
        qta_ref[0, head * QK_AUG:(head + 1) * QK_AUG, :] = jnp.concatenate(parts, axis=0).astype(BF16)


def _store_v_aug(vta_ref, vt):
    tm = vt.shape[1]
    pad = V_AUG - HEAD_DIM
    ones_row = jnp.where(lax.broadcasted_iota(jnp.int32, (pad, tm), 0) == 0, 1.0, 0.0)
    for j in range(N_KV):
        blk = jnp.concatenate([vt[j * HEAD_DIM:(j + 1) * HEAD_DIM, :], ones_row], axis=0)
        vta_ref[0, j * V_AUG:(j + 1) * V_AUG, :] = blk.astype(BF16)


def _pre_fox_prompt_kernel(x_ref, g_ref, wqt_ref, wk_ref, wkt_ref, wvt_ref, wf_ref, wft_ref, wz_ref,
                           bf_ref, bft_ref, low_ref, upp_ref, pk_ref, onek_ref,
                           qta_ref, ka_ref, vta_ref, kt_ref, vt_ref, lft_ref, gate_ref, ccarry, rcarry, *, per):
    @pl.when(pl.program_id(0) % per == 0)
    def _():
        ccarry[...] = jnp.zeros_like(ccarry)
        rcarry[...] = jnp.zeros_like(rcarry)

    tm = x_ref.shape[0]
    h = _rms_norm(x_ref[...], g_ref[...]).astype(BF16)
    lf = _log_sigmoid(_dot(h, wf_ref[...]) + bf_ref[...])
    lft = _log_sigmoid(_dot_nt(wft_ref[...], h) + bft_ref[...])
    lft_ref[0] = lft
    lp = _split3(lf)
    ccol = _dot(low_ref[...], lp[0]) + _dot(low_ref[...], lp[1]) + _dot(low_ref[...], lp[2]) + ccarry[...]
    crow = _dot3(_split3(lft), upp_ref[...]) + rcarry[...]
    ccarry[...] = ccol[tm - 1:tm, :]
    rcarry[...] = crow[:, tm - 1:tm]
    cp = _split3(ccol)
    ka = _dot(h, wk_ref[...]) + onek_ref[...]
    for n in range(N_PIECES):
        ka = ka + _dot(cp[n], pk_ref[n])
    ka_ref[...] = ka.astype(BF16)
    _store_q_aug(qta_ref, _dot_nt(wqt_ref[...], h), [p.astype(F32) for p in _split3(crow)])
    kt_ref[0] = _dot_nt(wkt_ref[...], h)
    vt = _dot_nt(wvt_ref[...], h)
    vt_ref[0] = vt
    _store_v_aug(vta_ref, vt)
    gate_ref[...] = _silu(_dot(h, wz_ref[...])).astype(BF16)


def _bias_placement():
    pk = np.zeros((N_PIECES, N_HEADS, K_AUG_LANES), np.float32)
    onek = np.zeros((1, K_AUG_LANES), np.float32)
    for head in range(N_HEADS):
        kvh, g = divmod(head, GROUP)
        slot = kvh * QK_AUG + HEAD_DIM + BIAS_SLOTS * g
        for n in range(N_PIECES):
            onek[0, slot + n] = 1.0
            pk[n, head, slot + N_PIECES + n] = -1.0
    return jnp.asarray(pk, BF16), jnp.asarray(onek)


def _attn_weights(wq, wk, wv):
    wk_pad = jnp.pad(wk.reshape(D_MODEL, N_KV, HEAD_DIM), ((0, 0), (0, 0), (0, QK_AUG - HEAD_DIM)))
    return dict(wqt=(wq * ATT_SCALE).T.astype(BF16), wk=wk_pad.reshape(D_MODEL, K_AUG_LANES).astype(BF16),
                wkt=wk.T.astype(BF16), wvt=wv.T.astype(BF16))


def _pre_fox_prompt(x, g, w, nb):
    rows = x.shape[0]
    seq = rows // nb
    tm = min(256, seq)
    per = seq // tm
    row, fmaj = _row_specs(rows, nb, tm)
    consts = [_tri(tm, False, False), _tri(tm, False, True), *_bias_placement()]
    names = ("wqt", "wk", "wkt", "wvt", "wf", "wft", "wz", "bf", "bft")
    return pl.pallas_call(
        functools.partial(_pre_fox_prompt_kernel, per=per),
        grid=(rows // tm,),
        in_specs=[row(D_MODEL), _full((1, D_MODEL))] + [_full(w[n].shape) for n in names]
                 + [_full(c.shape) for c in consts],
        out_specs=[fmaj(Q_AUG_ROWS), row(K_AUG_LANES), fmaj(N_KV * V_AUG), fmaj(KV_WIDTH), fmaj(KV_WIDTH),
                   fmaj(N_HEADS), row(D_MODEL)],
        out_shape=[jax.ShapeDtypeStruct((nb, Q_AUG_ROWS, seq), BF16), jax.ShapeDtypeStruct((rows, K_AUG_LANES), BF16),
                   jax.ShapeDtypeStruct((nb, N_KV * V_AUG, seq), BF16),
                   jax.ShapeDtypeStruct((nb, KV_WIDTH, seq), F32), jax.ShapeDtypeStruct((nb, KV_WIDTH, seq), F32),
                   jax.ShapeDtypeStruct((nb, N_HEADS, seq), F32), jax.ShapeDtypeStruct((rows, D_MODEL), BF16)],
        scratch_shapes=[pltpu.VMEM((1, N_HEADS), F32), pltpu.VMEM((N_HEADS, 1), F32)],
        compiler_params=_cparams("arbitrary"),
        name="pre_fox_prompt",
    )(x, g, *[w[n] for n in names], *consts)


def _pre_swa_prompt_kernel(x_ref, g_ref, wqt_ref, wk_ref, wkt_ref, wvt_ref, wz_ref,
                           cos_ref, sa_ref, sb_ref, cost_ref, sint_ref,
                           qta_ref, ka_ref, vta_ref, kt_ref, vt_ref, gate_ref):
    h = _rms_norm(x_ref[...], g_ref[...]).astype(BF16)
    cos_t, sin_t = cost_ref[...], sint_ref[...]
    _store_q_aug(qta_ref, _rope_fmaj(_dot_nt(wqt_ref[...], h), cos_t, sin_t, HEAD_DIM), None)
    ka_ref[...] = _rope_rows(_dot(h, wk_ref[...]), cos_ref[...], sa_ref[...], sb_ref[...]).astype(BF16)
    kt_ref[0] = _rope_fmaj(_dot_nt(wkt_ref[...], h), cos_t, sin_t, HEAD_DIM)
    vt = _dot_nt(wvt_ref[...], h)
    vt_ref[0] = vt
    _store_v_aug(vta_ref, vt)
    gate_ref[...] = _silu(_dot(h, wz_ref[...])).astype(BF16)


def _pre_swa_prompt(x, g, w, tables, nb):
    rows = x.shape[0]
    seq = rows // nb
    tm = min(256, seq)
    per = seq // tm
    row, fmaj = _row_specs(rows, nb, tm)
    tab = pl.BlockSpec((tm, LANES), lambda i: (i % per, 0))
    tab_t = pl.BlockSpec((ROT_HALF, tm), lambda i: (0, i % per))
    names = ("wqt", "wk", "wkt", "wvt", "wz")
    return pl.pallas_call(
        _pre_swa_prompt_kernel,
        grid=(rows // tm,),
        in_specs=[row(D_MODEL), _full((1, D_MODEL))] + [_full(w[n].shape) for n in names]
                 + [tab, tab, tab, tab_t, tab_t],
        out_specs=[fmaj(Q_AUG_ROWS), row(K_AUG_LANES), fmaj(N_KV * V_AUG), fmaj(KV_WIDTH), fmaj(KV_WIDTH),
                   row(D_MODEL)],
        out_shape=[jax.ShapeDtypeStruct((nb, Q_AUG_ROWS, seq), BF16), jax.ShapeDtypeStruct((rows, K_AUG_LANES), BF16),
                   jax.ShapeDtypeStruct((nb, N_KV * V_AUG, seq), BF16),
                   jax.ShapeDtypeStruct((nb, KV_WIDTH, seq), F32), jax.ShapeDtypeStruct((nb, KV_WIDTH, seq), F32),
                   jax.ShapeDtypeStruct((rows, D_MODEL), BF16)],
        compiler_params=_cparams("parallel"),
        name="pre_swa_prompt",
    )(x, g, *[w[n] for n in names], *tables)


def _pre_fox_decode_kernel(x_ref, g_ref, wq_ref, wkt_ref, wvt_ref, wft_ref, wz_ref, bft_ref,
                           q_ref, kt_ref, vt_ref, lft_ref, gate_ref):
    h = _rms_norm(x_ref[...], g_ref[...]).astype(BF16)
    q_ref[...] = _dot(h, wq_ref[...]).astype(BF16)
    kt_ref[0] = _dot_nt(wkt_ref[...], h)
    vt_ref[0] = _dot_nt(wvt_ref[...], h)
    lft_ref[0] = _log_sigmoid(_dot_nt(wft_ref[...], h) + bft_ref[...])
    gate_ref[...] = _silu(_dot(h, wz_ref[...])).astype(BF16)


def _pre_fox_decode(x, g, w):
    rows = x.shape[0]
    row, fmaj = _row_specs(rows, 1, rows)
    names = ("wq", "wkt", "wvt", "wft", "wz", "bft")
    return pl.pallas_call(
        _pre_fox_decode_kernel,
        grid=(1,),
        in_specs=[row(D_MODEL), _full((1, D_MODEL))] + [_full(w[n].shape) for n in names],
        out_specs=[row(D_MODEL), fmaj(KV_WIDTH), fmaj(KV_WIDTH), fmaj(N_HEADS), row(D_MODEL)],
        out_shape=[jax.ShapeDtypeStruct((rows, D_MODEL), BF16),
                   jax.ShapeDtypeStruct((1, KV_WIDTH, rows), F32), jax.ShapeDtypeStruct((1, KV_WIDTH, rows), F32),
                   jax.ShapeDtypeStruct((1, N_HEADS, rows), F32), jax.ShapeDtypeStruct((rows, D_MODEL), BF16)],
        compiler_params=_cparams("arbitrary"),
        name="pre_fox_decode",
    )(x, g, *[w[n] for n in names])


def _pre_swa_decode_kernel(x_ref, g_ref, wq_ref, wkt_ref, wvt_ref, wz_ref, cos_ref, sa_ref, sb_ref,
                           cost_ref, sint_ref, q_ref, kt_ref, vt_ref, gate_ref):
    h = _rms_norm(x_ref[...], g_ref[...]).astype(BF16)
    q_ref[...] = _rope_rows(_dot(h, wq_ref[...]), cos_ref[...], sa_ref[...], sb_ref[...]).astype(BF16)
    kt_ref[0] = _rope_fmaj(_dot_nt(wkt_ref[...], h), cost_ref[...], sint_ref[...], HEAD_DIM)
    vt_ref[0] = _dot_nt(wvt_ref[...], h)
    gate_ref[...] = _silu(_dot(h, wz_ref[...])).astype(BF16)


def _pre_swa_decode(x, g, w, tables):
    rows = x.shape[0]
    row, fmaj = _row_specs(rows, 1, rows)
    names = ("wq", "wkt", "wvt", "wz")
    return pl.pallas_call(
        _pre_swa_decode_kernel,
        grid=(1,),
        in_specs=[row(D_MODEL), _full((1, D_MODEL))] + [_full(w[n].shape) for n in names]
                 + [_full(t.shape) for t in tables],
        out_specs=[row(D_MODEL), fmaj(KV_WIDTH), fmaj(KV_WIDTH), row(D_MODEL)],
        out_shape=[jax.ShapeDtypeStruct((rows, D_MODEL), BF16),
                   jax.ShapeDtypeStruct((1, KV_WIDTH, rows), F32), jax.ShapeDtypeStruct((1, KV_WIDTH, rows), F32),
                   jax.ShapeDtypeStruct((rows, D_MODEL), BF16)],
        compiler_params=_cparams("arbitrary"),
        name="pre_swa_decode",
    )(x, g, *[w[n] for n in names], *tables)


def _pre_conv_kernel(x_ref, g_ref, wa_ref, wg_ref, wz_ref, glu_ref, gate_ref):
    h = _rms_norm(x_ref[...], g_ref[...]).astype(BF16)
    glu_ref[...] = _dot(h, wa_ref[...]) * _sigmoid(_dot(h, wg_ref[...]))
    gate_ref[...] = _silu(_dot(h, wz_ref[...])).astype(BF16)


def _pre_conv(x, g, w):
    rows = x.shape[0]
    tm = min(512, rows)
    row = pl.BlockSpec((tm, D_MODEL), lambda i: (i, 0))
    return pl.pallas_call(
        _pre_conv_kernel,
        grid=(rows // tm,),
        in_specs=[row, _full((1, D_MODEL)), _full(w["wa"].shape), _full(w["wg"].shape), _full(w["wz"].shape)],
        out_specs=[row, row],
        out_shape=[jax.ShapeDtypeStruct((rows, D_MODEL), F32), jax.ShapeDtypeStruct((rows, D_MODEL), BF16)],
        compiler_params=_cparams("parallel"),
        name="pre_conv",
    )(x, g, w["wa"], w["wg"], w["wz"])


def _post_mix_kernel(*refs, gated, final):
    refs = list(refs)
    a_ref = refs.pop(0)
    gate_ref = refs.pop(0) if gated else None
    x_ref, p_ref, wo_ref, ng_ref, wg_ref, bg_ref, wp_ref = refs[:7]
    fg_ref = refs[7] if final else None
    o_ref = refs[-1]
    a = a_ref[...]
    if gated:
        a = (a * gate_ref[...].astype(F32)).astype(BF16)
    x1 = x_ref[...] + _dot(a, wo_ref[...])
    h = _rms_norm(x1, ng_ref[...]).astype(BF16)
    gate2 = _sigmoid(_dot(h, wg_ref[...]) + bg_ref[...])
    x2 = x1 + gate2 * _dot(p_ref[...].astype(BF16), wp_ref[...])
    if final:
        x2 = _rms_norm(x2, fg_ref[...])
    o_ref[...] = x2


def _post_mix(a, gate, x, p_all, layer, w, final_g):
    rows = x.shape[0]
    tm = min(512, rows)
    row = lambda width: pl.BlockSpec((tm, width), lambda i: (i, 0))
    gated, final = gate is not None, final_g is not None
    pe = p_all.shape[2]
    args = [a] + ([gate] if gated else []) + [x, p_all, w["wo"], w["ng"], w["wg"], w["bg"], w["wp"]]
    specs = [row(D_MODEL)] * (2 if gated else 1)
    specs += [row(D_MODEL), pl.BlockSpec((None, tm, pe), lambda i: (layer, i, 0)),
              _full((D_MODEL, D_MODEL)), _full((1, D_MODEL)), _full((D_MODEL, D_MODEL)),
              _full((1, D_MODEL)), _full((pe, D_MODEL))]
    if final:
        args.append(final_g)
        specs.append(_full((1, D_MODEL)))
    return pl.pallas_call(
        functools.partial(_post_mix_kernel, gated=gated, final=final),
        grid=(rows // tm,),
        in_specs=specs,
        out_specs=row(D_MODEL),
        out_shape=jax.ShapeDtypeStruct((rows, D_MODEL), F32),
        compiler_params=_cparams("parallel"),
        name="post_mix",
    )(*args)


def _softmax_step(st, m, acc, vta):
    m_new = jnp.maximum(m, jnp.max(st, axis=0, keepdims=True))
    alpha = jnp.exp(m - m_new)
    p = jnp.exp(st - m_new).astype(BF16)
    return m_new, alpha * acc + _dot(vta, p)


def _finish_heads(accs, denoms, gate_ref, o_ref):
    outs = [(acc[0:HEAD_DIM, :] / den).T for acc, den in zip(accs, denoms)]
    o_ref[...] = (jnp.concatenate(outs, axis=1) * gate_ref[...].astype(F32)).astype(BF16)


def _fox_prompt_kernel(qta_ref, ka_ref, vta_ref, gate_ref, o_ref, *, tq):
    i = pl.program_id(2)
    half = tq // 2
    qts = [qta_ref[0, g * QK_AUG:(g + 1) * QK_AUG, :] for g in range(GROUP)]

    def step(j, carry):
        off = pl.multiple_of(j * tq, tq)
        ka = ka_ref[pl.ds(off, tq), :]
        vta = vta_ref[0, :, pl.ds(off, tq)]
        sts = [_dot(ka, qts[g]) for g in range(GROUP)]
        return tuple(_softmax_step(sts[g], carry[g][0], carry[g][1], vta) for g in range(GROUP))

    init = tuple((jnp.full((1, tq), -jnp.inf, F32), jnp.zeros((V_AUG, tq), F32)) for _ in range(GROUP))
    carry = lax.fori_loop(0, i, step, init)

    off = pl.multiple_of(i * tq, tq)
    ka = ka_ref[pl.ds(off, tq), :]
    vta = vta_ref[0, :, pl.ds(off, tq)]
    tri_a = (lax.broadcasted_iota(jnp.int32, (half, half), 0) <= lax.broadcasted_iota(jnp.int32, (half, half), 1))
    tri_b = (lax.broadcasted_iota(jnp.int32, (tq, half), 0)
             <= lax.broadcasted_iota(jnp.int32, (tq, half), 1) + half)
    sts_a = [jnp.where(tri_a, _dot(ka[0:half, :], qts[g][:, 0:half]), -jnp.inf) for g in range(GROUP)]
    sts_b = [jnp.where(tri_b, _dot(ka, qts[g][:, half:]), -jnp.inf) for g in range(GROUP)]
    accs = []
    for g in range(GROUP):
        m, acc = carry[g]
        _, acc_a = _softmax_step(sts_a[g], m[:, 0:half], acc[:, 0:half], vta[:, 0:half])
        _, acc_b = _softmax_step(sts_b[g], m[:, half:], acc[:, half:], vta)
        accs.append(jnp.concatenate([acc_a, acc_b], axis=1))
    _finish_heads(accs, [acc[HEAD_DIM:HEAD_DIM + 1, :] for acc in accs], gate_ref, o_ref)


def _fox_prompt(qta, ka, vta, gate):
    batch, _, seq = qta.shape
    rows = batch * seq
    tq = min(512, seq)
    assert seq % tq == 0 and tq % (2 * LANES) == 0
    nq = seq // tq
    width = GROUP * HEAD_DIM
    qspec = pl.BlockSpec((tq, width), lambda b, h, i: (b * nq + i, h))
    return pl.pallas_call(
        functools.partial(_fox_prompt_kernel, tq=tq),
        grid=(batch, N_KV, nq),
        in_specs=[pl.BlockSpec((1, GROUP * QK_AUG, tq), lambda b, h, i: (b, h, i)),
                  pl.BlockSpec((seq, QK_AUG), lambda b, h, i: (b, h)),
                  pl.BlockSpec((1, V_AUG, seq), lambda b, h, i: (b, h, 0)),
                  qspec],
        out_specs=qspec,
        out_shape=jax.ShapeDtypeStruct((rows, D_MODEL), BF16),
        compiler_params=_cparams("parallel", "parallel", "arbitrary"),
        name="fox_prompt",
    )(qta, ka, vta, gate)


SWA_QB = 4


def _swa_prompt_kernel(qta_ref, ka_ref, vta_ref, sink_ref, gate_ref, o_ref):
    n0 = pl.program_id(2) * SWA_QB
    ki = lax.broadcasted_iota(jnp.int32, (2 * Q_BLOCK, Q_BLOCK), 0)
    qi = lax.broadcasted_iota(jnp.int32, (2 * Q_BLOCK, Q_BLOCK), 1)
    starts, sts = [], []
    for r in range(SWA_QB):
        n = n0 + r
        start = pl.multiple_of(jnp.maximum(n - 1, 0) * Q_BLOCK, Q_BLOCK)
        rel = (n * Q_BLOCK + qi) - (start + ki)
        allowed = (rel >= 0) & (rel < WINDOW)
        ka = ka_ref[pl.ds(start, 2 * Q_BLOCK), :]
        starts.append(start)
        sts.append([jnp.where(allowed, _dot(ka, qta_ref[0, g * QK_AUG:(g + 1) * QK_AUG,
                                                        r * Q_BLOCK:(r + 1) * Q_BLOCK]), -jnp.inf)
                    for g in range(GROUP)])
    sinks = [sink_ref[0, :, g:g + 1] for g in range(GROUP)]
    ms = [[jnp.maximum(jnp.max(sts[r][g], axis=0, keepdims=True), sinks[g]) for g in range(GROUP)]
          for r in range(SWA_QB)]
    ps = [[jnp.exp(sts[r][g] - ms[r][g]).astype(BF16) for g in range(GROUP)] for r in range(SWA_QB)]
    for r in range(SWA_QB):
        vta = vta_ref[0, :, pl.ds(starts[r], 2 * Q_BLOCK)]
        accs = [_dot(vta, ps[r][g]) for g in range(GROUP)]
        denoms = [accs[g][HEAD_DIM:HEAD_DIM + 1, :] + jnp.exp(sinks[g] - ms[r][g]) for g in range(GROUP)]
        rows = slice(r * Q_BLOCK, (r + 1) * Q_BLOCK)
        _finish_heads(accs, denoms, gate_ref.at[rows, :], o_ref.at[rows, :])


def _swa_prompt(qta, ka, vta, sinks, gate):
    batch, _, seq = qta.shape
    rows = batch * seq
    tq = SWA_QB * Q_BLOCK
    nq = seq // tq
    width = GROUP * HEAD_DIM
    qspec = pl.BlockSpec((tq, width), lambda b, h, i: (b * nq + i, h))
    return pl.pallas_call(
        _swa_prompt_kernel,
        grid=(batch, N_KV, nq),
        in_specs=[pl.BlockSpec((1, GROUP * QK_AUG, tq), lambda b, h, i: (b, h, i)),
                  pl.BlockSpec((seq, QK_AUG), lambda b, h, i: (b, h)),
                  pl.BlockSpec((1, V_AUG, seq), lambda b, h, i: (b, h, 0)),
                  pl.BlockSpec((1, 1, GROUP), lambda b, h, i: (h, 0, 0)),
                  qspec],
        out_specs=qspec,
        out_shape=jax.ShapeDtypeStruct((rows, D_MODEL), BF16),
        compiler_params=_cparams("parallel", "parallel", "arbitrary"),
        name="swa_prompt",
    )(qta, ka, vta, sinks.reshape(N_KV, 1, GROUP), gate)


def _block_diag_mask():
    r = lax.broadcasted_iota(jnp.int32, (DEC_ROWS, KV_WIDTH), 0)
    c = lax.broadcasted_iota(jnp.int32, (DEC_ROWS, KV_WIDTH), 1)
    return ((r % N_HEADS) // GROUP) == (c // HEAD_DIM)


def _fold_heads(o_full):
    om = jnp.where(_block_diag_mask(), o_full, 0.0)
    return (om[:, 0:HEAD_DIM] + om[:, HEAD_DIM:2 * HEAD_DIM]
            + om[:, 2 * HEAD_DIM:3 * HEAD_DIM] + om[:, 3 * HEAD_DIM:4 * HEAD_DIM])


def _new_token_mask():
    lane = lax.broadcasted_iota(jnp.int32, (DEC_ROWS, PAGE), 1)
    t = lax.broadcasted_iota(jnp.int32, (DEC_ROWS, PAGE), 0) // N_HEADS
    return (lane >= NEW_LANE) & (lane - NEW_LANE <= t)


def _fox_decode_kernel(pt_ref, qbd_ref, knew_ref, vnew_ref, lfnew_ref, kc_ref, vc_ref, lc_ref, o_ref,
                       kbuf, vbuf, lbuf, sems, *, layer, cp):
    b = pl.program_id(0)
    nb = pl.num_programs(0)
    nch = pt_ref.shape[1] // cp

    def copies(seq, ci, slot):
        base = (nch - 1 - ci) * cp
        out = []
        for p in range(cp):
            page = pt_ref[seq, base + p]
            out.append(pltpu.make_async_copy(kc_ref.at[layer, page], kbuf.at[slot, p], sems.at[0, slot]))
            out.append(pltpu.make_async_copy(vc_ref.at[layer, page], vbuf.at[slot, p], sems.at[1, slot]))
            out.append(pltpu.make_async_copy(lc_ref.at[layer, page], lbuf.at[slot, p], sems.at[2, slot]))
        return out

    def start(descs):
        for n, c in enumerate(descs):
            c.start(priority=n % 3 % 2)

    @pl.when(b == 0)
    def _():
        start(copies(0, 0, 0))

    qbd = qbd_ref[0]

    x = lfnew_ref[0]
    cn = x + pltpu.roll(x, 1, 1) + pltpu.roll(x, 2, 1) + pltpu.roll(x, 3, 1)
    s_new = _dot(qbd, knew_ref[0].astype(BF16))
    cqs = [cn[:, NEW_LANE + t:NEW_LANE + t + 1] for t in range(DEC_T)]
    s_new = s_new + jnp.concatenate([cqs[t] - cn for t in range(DEC_T)], axis=0)
    s_new = jnp.where(_new_token_mask(), s_new, -jnp.inf)
    cq = jnp.concatenate(cqs, axis=0)
    m = jnp.max(s_new, axis=-1, keepdims=True)
    p = jnp.exp(s_new - m)
    l = jnp.sum(p, axis=-1, keepdims=True)
    acc = _dot_nt(p.astype(BF16), vnew_ref[0].astype(BF16))

    sfx = jnp.concatenate([_tri(PAGE, strict=True, upper=False), jnp.ones((PAGE, LANES), BF16)], axis=1)

    def chunk(ci, carry):
        m, l, acc, later = carry
        slot = (b * nch + ci) % 2

        @pl.when(ci + 1 < nch)
        def _():
            start(copies(b, ci + 1, 1 - slot))

        @pl.when(jnp.logical_and(ci + 1 == nch, b + 1 < nb))
        def _():
            start(copies(b + 1, 0, 1 - slot))

        for c in copies(b, ci, slot):
            c.wait()

        st = _dot3(_split3(lbuf[slot].reshape(cp * N_HEADS, PAGE)), sfx).reshape(cp, N_HEADS, 2 * LANES)
        bias = [None] * cp
        for pg in reversed(range(cp)):
            bias[pg] = jnp.concatenate([st[pg, :, :LANES] + later] * DEC_T, axis=0) + cq
            later = later + st[pg, :, LANES:]
        ss = [_dot(qbd, kbuf[slot, pg].astype(BF16)) + bias[pg] for pg in range(cp)]
        per = cp // DEC_GROUPS if cp % DEC_GROUPS == 0 else cp
        parts = []
        for lo in range(0, cp, per):
            grp = ss[lo:lo + per]
            m_loc = jnp.max(functools.reduce(jnp.maximum, grp), axis=-1, keepdims=True)
            ps = [jnp.exp(s - m_loc) for s in grp]
            l_loc = jnp.sum(functools.reduce(jnp.add, ps), axis=-1, keepdims=True)
            parts.append((m_loc, l_loc, lo, ps))
        accs = []
        for m_loc, l_loc, lo, ps in parts:
            a = _dot_nt(ps[0].astype(BF16), vbuf[slot, lo].astype(BF16))
            for n in range(1, len(ps)):
                a = a + _dot_nt(ps[n].astype(BF16), vbuf[slot, lo + n].astype(BF16))
            accs.append(a)
        m_new = functools.reduce(jnp.maximum, [m] + [pt[0] for pt in parts])
        alpha = jnp.exp(m - m_new)
        l, acc = alpha * l, alpha * acc
        for (m_loc, l_loc, _, _), a in zip(parts, accs):
            w = jnp.exp(m_loc - m_new)
            l, acc = l + w * l_loc, acc + w * a
        return m_new, l, acc, later

    m, l, acc, _ = lax.fori_loop(0, nch, chunk, (m, l, acc, jnp.zeros((N_HEADS, LANES), F32)))
    o_ref[0] = _fold_heads(acc / l)


def _fox_decode(page_table, qbd, knew, vnew, lfnew, cache_kt, cache_vt, cache_lft, layer):
    batch, n_pages = page_table.shape
    cp = 16
    while n_pages % cp:
        cp //= 2
    seq3 = lambda r, c: pl.BlockSpec((1, r, c), lambda b, pt: (b, 0, 0))
    grid_spec = pltpu.PrefetchScalarGridSpec(
        num_scalar_prefetch=1,
        grid=(batch,),
        in_specs=[seq3(DEC_ROWS, KV_WIDTH), seq3(KV_WIDTH, PAGE), seq3(KV_WIDTH, PAGE), seq3(N_HEADS, PAGE),
                  pl.BlockSpec(memory_space=pl.ANY), pl.BlockSpec(memory_space=pl.ANY),
                  pl.BlockSpec(memory_space=pl.ANY)],
        out_specs=seq3(DEC_ROWS, HEAD_DIM),
        scratch_shapes=[pltpu.VMEM((2, cp, KV_WIDTH, PAGE), F32), pltpu.VMEM((2, cp, KV_WIDTH, PAGE), F32),
                        pltpu.VMEM((2, cp, N_HEADS, PAGE), F32), pltpu.SemaphoreType.DMA((3, 2))],
    )
    return pl.pallas_call(
        functools.partial(_fox_decode_kernel, layer=layer, cp=cp),
        grid_spec=grid_spec,
        out_shape=jax.ShapeDtypeStruct((batch, DEC_ROWS, HEAD_DIM), F32),
        compiler_params=_cparams("arbitrary"),
        name="fox_decode",
    )(page_table, qbd, knew, vnew, lfnew, cache_kt, cache_vt, cache_lft)


def _swa_decode_kernel(qbd_ref, knew_ref, vnew_ref, sk_ref, sv_ref, sink_ref, o_ref, nk_ref, nv_ref):
    qbd = qbd_ref[0]
    sk, sv, kn, vn = sk_ref[0, 0], sv_ref[0, 0], knew_ref[0], vnew_ref[0]
    lane = lax.broadcasted_iota(jnp.int32, (DEC_ROWS, PAGE), 1)
    t = lax.broadcasted_iota(jnp.int32, (DEC_ROWS, PAGE), 0) // N_HEADS
    s_old = jnp.where(lane > t, _dot(qbd, sk.astype(BF16)), -jnp.inf)
    s_new = jnp.where(_new_token_mask(), _dot(qbd, kn.astype(BF16)), -jnp.inf)
    sink = sink_ref[...]
    m = jnp.maximum(jnp.maximum(jnp.max(s_old, axis=-1, keepdims=True), jnp.max(s_new, axis=-1, keepdims=True)),
                    sink)
    p_old = jnp.exp(s_old - m)
    p_new = jnp.exp(s_new - m)
    denom = (jnp.sum(p_old, axis=-1, keepdims=True) + jnp.sum(p_new, axis=-1, keepdims=True)
             + jnp.exp(sink - m))
    o_full = _dot_nt(p_old.astype(BF16), sv.astype(BF16)) + _dot_nt(p_new.astype(BF16), vn.astype(BF16))
    o_ref[0] = _fold_heads(o_full / denom)
    keep_new = lax.broadcasted_iota(jnp.int32, (KV_WIDTH, PAGE), 1) >= NEW_LANE
    nk_ref[0] = jnp.where(keep_new, kn, pltpu.roll(sk, NEW_LANE, 1))
    nv_ref[0] = jnp.where(keep_new, vn, pltpu.roll(sv, NEW_LANE, 1))


def _swa_decode(qbd, knew, vnew, state_kt, state_vt, sink_rows, layer):
    batch, buf = state_kt.shape[1], state_kt.shape[3]
    assert buf == WINDOW == PAGE
    seq3 = lambda r, c: pl.BlockSpec((1, r, c), lambda b: (b, 0, 0))
    st = pl.BlockSpec((1, 1, KV_WIDTH, buf), lambda b: (layer, b, 0, 0))
    return pl.pallas_call(
        _swa_decode_kernel,
        grid=(batch,),
        in_specs=[seq3(DEC_ROWS, KV_WIDTH), seq3(KV_WIDTH, PAGE), seq3(KV_WIDTH, PAGE), st, st,
                  _full((DEC_ROWS, 1))],
        out_specs=[seq3(DEC_ROWS, HEAD_DIM), seq3(KV_WIDTH, buf), seq3(KV_WIDTH, buf)],
        out_shape=[jax.ShapeDtypeStruct((batch, DEC_ROWS, HEAD_DIM), F32),
                   jax.ShapeDtypeStruct((batch, KV_WIDTH, buf), F32),
                   jax.ShapeDtypeStruct((batch, KV_WIDTH, buf), F32)],
        compiler_params=_cparams("parallel"),
        name="swa_decode",
    )(qbd, knew, vnew, state_kt, state_vt, sink_rows)


HALO = 32
CONV_CHUNK = 32


def _conv_tail(y, gate, lng, lnb):
    yc = y - jnp.mean(y, axis=-1, keepdims=True)
    yn = yc * lax.rsqrt(jnp.mean(yc * yc, axis=-1, keepdims=True) + NORM_EPS) * lng + lnb
    return (_silu(yn) * gate.astype(F32)).astype(BF16)


def _conv_prompt_kernel(x_ref, g_ref, wa_ref, wg_ref, wz_ref, dw_ref, db_ref, lng_ref, lnb_ref,
                        a_ref, st_ref, buf, shifted, gbuf):
    i = pl.program_id(1)
    tm = x_ref.shape[0]
    h = _rms_norm(x_ref[...], g_ref[...]).astype(BF16)
    glu = _dot(h, wa_ref[...]) * _sigmoid(_dot(h, wg_ref[...]))
    gate = _silu(_dot(h, wz_ref[...]))

    @pl.when(i == 0)
    def _():
        buf[0:HALO, :] = jnp.zeros((HALO, D_MODEL), F32)

    buf[HALO:HALO + tm, :] = glu
    gbuf[...] = gate
    first = HALO - CONV_HIST
    for r in range(SUBLANES):
        span = tm + SUBLANES * ((CONV_HIST - r) // SUBLANES)
        shifted[r, 0:span, :] = buf[first + r:first + r + span, :]

    def chunk(c, _):
        off = pl.multiple_of(c * CONV_CHUNK, CONV_CHUNK)
        y = jnp.zeros((CONV_CHUNK, D_MODEL), F32) + db_ref[...]
        for w in range(CONV_WIDTH):
            start = off + SUBLANES * (w // SUBLANES)
            y = y + shifted[w % SUBLANES, pl.ds(start, CONV_CHUNK), :] * dw_ref[w:w + 1, :]
        rows = pl.ds(off, CONV_CHUNK)
        a_ref[rows, :] = _conv_tail(y, gbuf[rows, :], lng_ref[...], lnb_ref[...])
        return 0

    lax.fori_loop(0, tm // CONV_CHUNK, chunk, 0)

    @pl.when(i == pl.num_programs(1) - 1)
    def _():
        st_ref[0] = buf[HALO + tm - CONV_HIST:HALO + tm, :]

    buf[0:HALO, :] = buf[tm:tm + HALO, :]


def _conv_prompt(x, g, w, batch, seq):
    rows = batch * seq
    tm = min(256, seq)
    nt = seq // tm
    row = pl.BlockSpec((tm, D_MODEL), lambda b, i: (b * nt + i, 0))
    return pl.pallas_call(
        _conv_prompt_kernel,
        grid=(batch, nt),
        in_specs=[row, _full((1, D_MODEL)), _full(w["wa"].shape), _full(w["wg"].shape), _full(w["wz"].shape),
                  _full((CONV_WIDTH, D_MODEL)), _full((1, D_MODEL)), _full((1, D_MODEL)), _full((1, D_MODEL))],
        out_specs=[row, pl.BlockSpec((1, CONV_HIST, D_MODEL), lambda b, i: (b, 0, 0))],
        out_shape=[jax.ShapeDtypeStruct((rows, D_MODEL), BF16),
                   jax.ShapeDtypeStruct((batch, CONV_HIST, D_MODEL), F32)],
        scratch_shapes=[pltpu.VMEM((HALO + tm, D_MODEL), F32),
                        pltpu.VMEM((SUBLANES, tm + SUBLANES * (CONV_HIST // SUBLANES), D_MODEL), F32),
                        pltpu.VMEM((tm, D_MODEL), F32)],
        compiler_params=_cparams("arbitrary", "arbitrary"),
        name="conv_prompt",
    )(x, g, w["wa"], w["wg"], w["wz"], w["dw"], w["db"], w["lng"], w["lnb"])


def _conv_decode_kernel(glu_ref, gate_ref, st_ref, dw_ref, db_ref, lng_ref, lnb_ref, a_ref, ns_ref, pad):
    sb = glu_ref.shape[1]
    pad[0:CONV_HIST] = st_ref[0]
    pad[CONV_HIST:CONV_HIST + DEC_T] = glu_ref[...]
    y = jnp.zeros((DEC_T, sb, D_MODEL), F32) + db_ref[...]
    for w in range(CONV_WIDTH):
        y = y + pad[w:w + DEC_T] * dw_ref[w:w + 1, :]
    a_ref[...] = _conv_tail(y, gate_ref[...], lng_ref[...], lnb_ref[...])
    ns_ref[...] = pad[DEC_T:DEC_T + CONV_HIST]


def _conv_decode(glu, gate, state, w, layer):
    batch = state.shape[2]
    sb = 16
    while batch % sb:
        sb //= 2
    tok = pl.BlockSpec((DEC_T, sb, D_MODEL), lambda i: (0, i, 0))
    return pl.pallas_call(
        _conv_decode_kernel,
        grid=(batch // sb,),
        in_specs=[tok, tok, pl.BlockSpec((1, CONV_HIST, sb, D_MODEL), lambda i: (layer, 0, i, 0)),
                  _full((CONV_WIDTH, D_MODEL)), _full((1, D_MODEL)), _full((1, D_MODEL)), _full((1, D_MODEL))],
        out_specs=[tok, pl.BlockSpec((CONV_HIST, sb, D_MODEL), lambda i: (0, i, 0))],
        out_shape=[jax.ShapeDtypeStruct((DEC_T, batch, D_MODEL), BF16),
                   jax.ShapeDtypeStruct((CONV_HIST, batch, D_MODEL), F32)],
        scratch_shapes=[pltpu.VMEM((CONV_HIST + DEC_T, sb, D_MODEL), F32)],
        compiler_params=_cparams("parallel"),
        name="conv_decode",
    )(glu.reshape(DEC_T, batch, D_MODEL), gate.reshape(DEC_T, batch, D_MODEL), state,
      w["dw"], w["db"], w["lng"], w["lnb"])


def _rope_tables(pos):
    inv_freq = ROPE_THETA ** (-jnp.arange(ROT_HALF, dtype=F32) / ROT_HALF)
    ang = pos.astype(F32)[:, None] * inv_freq[None, :]
    cos, sin = jnp.cos(ang), jnp.sin(ang)
    n = pos.shape[0]
    rest = jnp.zeros((n, HEAD_DIM - ROT_DIM), F32)
    zero = jnp.zeros((n, ROT_HALF), F32)
    cos_h = jnp.concatenate([cos, cos, rest + 1.0], axis=1)
    sa_h = jnp.concatenate([-sin, zero, rest], axis=1)
    sb_h = jnp.concatenate([zero, sin, rest], axis=1)
    rep = LANES // HEAD_DIM
    return tuple(jnp.tile(t, (1, rep)) for t in (cos_h, sa_h, sb_h)) + (cos.T, sin.T)


def _decode_queries(q_rows, batch):
    q4 = q_rows.reshape(DEC_T, batch, N_HEADS, 1, HEAD_DIM).transpose(1, 0, 2, 3, 4)
    own = (jnp.arange(N_HEADS)[:, None] // GROUP) == jnp.arange(N_KV)[None, :]
    return jnp.where(own[None, None, :, :, None], q4, jnp.zeros((), q_rows.dtype)).reshape(batch, DEC_ROWS, KV_WIDTH)


def _new_token_pages(xt, batch):
    width = xt.shape[1]
    per_seq = xt.reshape(width, DEC_T, batch).transpose(2, 0, 1)
    return jnp.pad(per_seq, ((0, 0), (0, 0), (NEW_LANE, 0)))


def _decode_out_rows(o, batch):
    return o.reshape(batch, DEC_T, D_MODEL).transpose(1, 0, 2).reshape(DEC_T * batch, D_MODEL)


def _sample_major(xt, batch, inner):
    return xt.reshape(inner + (DEC_T, batch)).transpose((len(inner) + 1, len(inner)) + tuple(range(len(inner))))


def kernel(x_prompt, x_sample, cache_fox_k, cache_fox_v, cache_fox_lf, page_table, state_swa_k, state_swa_v,
           state_conv, p_prompt, p_sample, norm_g, fox_w_in, fox_b_f, fox_w_out, swa_w_in, swa_sinks, swa_w_out,
           conv_w_in, conv_dw_w, conv_dw_b, conv_ln_g, conv_ln_b, conv_w_out, pe_w_proj, pe_norm_g, pe_w_gate,
           pe_b_gate, final_norm_g):
    batch, seq, _ = x_prompt.shape
    dec_batch, dec_t, _ = x_sample.shape
    assert dec_t == DEC_T and seq % (SWA_QB * Q_BLOCK) == 0
    depth = norm_g.shape[0]
    past_len = page_table.shape[1] * PAGE
    buf = state_swa_k.shape[2]
    rows_p, rows_s = batch * seq, dec_batch * dec_t
    att = N_HEADS * HEAD_DIM

    xp = x_prompt.reshape(rows_p, D_MODEL)
    xs = x_sample.transpose(1, 0, 2).reshape(rows_s, D_MODEL)
    pp_all = p_prompt.reshape(depth, rows_p, -1)
    ps_all = p_sample.transpose(0, 2, 1, 3).reshape(depth, rows_s, -1)
    fmaj = lambda c: c.transpose(0, 1, 3, 4, 2).reshape(c.shape[:2] + (KV_WIDTH, c.shape[2]))
    cache_kt, cache_vt = fmaj(cache_fox_k), fmaj(cache_fox_v)
    cache_lft = cache_fox_lf.transpose(0, 1, 3, 2)
    st_kt, st_vt = fmaj(state_swa_k), fmaj(state_swa_v)
    st_conv = state_conv.transpose(0, 2, 1, 3)
    heads_major = lambda t: t.reshape(t.shape[0], N_KV, HEAD_DIM, t.shape[2]).transpose(0, 3, 1, 2)

    fox_k_p, fox_v_p, fox_lf_p, fox_k_s, fox_v_s, fox_lf_s = [], [], [], [], [], []
    swa_k_p, swa_v_p, swa_k_s, swa_v_s = [], [], [], []
    conv_p, conv_s = [], []

    for i in range(depth):
        kind, j = i % N_MIXERS, i // N_MIXERS
        g = norm_g[i].reshape(1, D_MODEL)
        if kind == 0:
            w_in = fox_w_in[j]
            wq, wk = w_in[:, :att], w_in[:, att:att + KV_WIDTH]
            wv = w_in[:, att + KV_WIDTH:att + 2 * KV_WIDTH]
            wf = w_in[:, att + 2 * KV_WIDTH:att + 2 * KV_WIDTH + N_HEADS]
            w = dict(_attn_weights(wq, wk, wv), wq=(wq * ATT_SCALE).astype(BF16),
                     wf=wf.astype(BF16), wft=wf.T.astype(BF16),
                     wz=w_in[:, att + 2 * KV_WIDTH + N_HEADS:].astype(BF16),
                     bf=fox_b_f[j].reshape(1, N_HEADS), bft=fox_b_f[j].reshape(N_HEADS, 1))
            w_out = fox_w_out[j]
            qta, ka, vta, kt, vt, lft, gate = _pre_fox_prompt(xp, g, w, batch)
            ap = _fox_prompt(qta, ka, vta, gate)
            gate_p = None
            fox_k_p.append(heads_major(kt))
            fox_v_p.append(heads_major(vt))
            fox_lf_p.append(lft.transpose(0, 2, 1))
            q, kt, vt, lft, gate_s = _pre_fox_decode(xs, g, w)
            o = _fox_decode(page_table, _decode_queries(q, dec_batch), _new_token_pages(kt, dec_batch),
                            _new_token_pages(vt, dec_batch), _new_token_pages(lft, dec_batch),
                            cache_kt, cache_vt, cache_lft, j)
            a_s = _decode_out_rows(o, dec_batch)
            fox_k_s.append(_sample_major(kt, dec_batch, (N_KV, HEAD_DIM)))
            fox_v_s.append(_sample_major(vt, dec_batch, (N_KV, HEAD_DIM)))
            fox_lf_s.append(_sample_major(lft, dec_batch, (N_HEADS,)))
        elif kind == 1:
            w_in = swa_w_in[j]
            wq, wk = w_in[:, :att], w_in[:, att:att + KV_WIDTH]
            wv = w_in[:, att + KV_WIDTH:att + 2 * KV_WIDTH]
            w = dict(_attn_weights(wq, wk, wv), wq=(wq * ATT_SCALE).astype(BF16),
                     wz=w_in[:, att + 2 * KV_WIDTH:].astype(BF16))
            w_out = swa_w_out[j]
            tables = _rope_tables(jnp.arange(seq, dtype=jnp.int32))
            qta, ka, vta, kt, vt, gate = _pre_swa_prompt(xp, g, w, tables, batch)
            ap = _swa_prompt(qta, ka, vta, swa_sinks[j], gate)
            gate_p = None
            swa_k_p.append(heads_major(kt[:, :, seq - buf:]))
            swa_v_p.append(heads_major(vt[:, :, seq - buf:]))
            pos_s = past_len + jnp.repeat(jnp.arange(DEC_T, dtype=jnp.int32), dec_batch)
            q, kt, vt, gate_s = _pre_swa_decode(xs, g, w, _rope_tables(pos_s))
            sink_rows = jnp.tile(swa_sinks[j].astype(F32), DEC_T).reshape(DEC_ROWS, 1)
            o, nkt, nvt = _swa_decode(_decode_queries(q, dec_batch), _new_token_pages(kt, dec_batch),
                                      _new_token_pages(vt, dec_batch), st_kt, st_vt, sink_rows, j)
            a_s = _decode_out_rows(o, dec_batch)
            swa_k_s.append(heads_major(nkt))
            swa_v_s.append(heads_major(nvt))
        else:
            w_in = conv_w_in[j]
            cd = w_in.shape[1] // 3
            w = dict(wa=w_in[:, :cd].astype(BF16), wg=w_in[:, cd:2 * cd].astype(BF16),
                     wz=w_in[:, 2 * cd:].astype(BF16), dw=conv_dw_w[j], db=conv_dw_b[j].reshape(1, cd),
                     lng=conv_ln_g[j].reshape(1, cd), lnb=conv_ln_b[j].reshape(1, cd))
            w_out = conv_w_out[j]
            ap, st = _conv_prompt(xp, g, w, batch, seq)
            gate_p = None
            conv_p.append(st)
            glu, gate = _pre_conv(xs, g, w)
            a_s, ns = _conv_decode(glu, gate, st_conv, w, j)
            a_s = a_s.reshape(rows_s, D_MODEL)
            gate_s = None
            conv_s.append(ns.transpose(1, 0, 2))
        wpost = dict(wo=w_out.astype(BF16), ng=pe_norm_g[i].reshape(1, D_MODEL), wg=pe_w_gate[i].astype(BF16),
                     bg=pe_b_gate[i].reshape(1, D_MODEL), wp=pe_w_proj[i].astype(BF16))
        fg = final_norm_g.reshape(1, D_MODEL) if i == depth - 1 else None
        xp = _post_mix(ap, gate_p, xp, pp_all, i, wpost, fg)
        xs = _post_mix(a_s, gate_s, xs, ps_all, i, wpost, fg)

    return (xp.reshape(batch, seq, D_MODEL), xs.reshape(DEC_T, dec_batch, D_MODEL).transpose(1, 0, 2),
            jnp.stack(fox_k_p), jnp.stack(fox_v_p), jnp.stack(fox_lf_p),
            jnp.stack(fox_k_s), jnp.stack(fox_v_s), jnp.stack(fox_lf_s),
            jnp.stack(swa_k_p), jnp.stack(swa_v_p), jnp.stack(swa_k_s), jnp.stack(swa_v_s),
            jnp.stack(conv_p), jnp.stack(conv_s))
```

```python
import functools

import numpy as np

import jax
import jax.numpy as jnp
from jax import lax
from jax.experimental import pallas as pl
from jax.experimental.pallas import tpu as pltpu

F32 = jnp.float32
BF16 = jnp.bfloat16

D_MODEL = 1024
HEAD_DIM = 64
N_HEADS = 16
N_KV = 4
GROUP = 4
KV_WIDTH = N_KV * HEAD_DIM
ATT_SCALE = HEAD_DIM ** -0.5
N_MIXERS = 3
Q_BLOCK = 128
WINDOW = 128
ROPE_THETA = 500000.0
ROT_DIM = HEAD_DIM // 4
ROT_HALF = ROT_DIM // 2
CONV_WIDTH = 31
CONV_HIST = CONV_WIDTH - 1
NORM_EPS = 1e-6
PAGE = 128
LANES = 128
SUBLANES = 8
VMEM_LIMIT = 52 * 1024 * 1024
DEC_T = 4
DEC_ROWS = DEC_T * N_HEADS
NEW_LANE = PAGE - DEC_T
DEC_SLOTS = 3
DEC_GROUPS = 4

QK_AUG = 128
BIAS_SLOTS = 8
N_PIECES = 3
V_AUG = 80
Q_AUG_ROWS = N_HEADS * QK_AUG
K_AUG_LANES = N_KV * QK_AUG

NT_DIMS = (((1,), (1,)), ((), ()))


def _cparams(*sem, flags=None):
    return pltpu.CompilerParams(dimension_semantics=sem, vmem_limit_bytes=VMEM_LIMIT, flags=flags)


def _dot(a, b):
    return jnp.dot(a, b, preferred_element_type=F32)


def _dot_nt(a, b):
    return lax.dot_general(a, b, NT_DIMS, preferred_element_type=F32)


def _sigmoid(x):
    return 1.0 / (1.0 + jnp.exp(-x))


def _silu(x):
    return x * _sigmoid(x)


def _log_sigmoid(x):
    return jnp.minimum(x, 0.0) - jnp.log1p(jnp.exp(-jnp.abs(x)))


def _rms_norm(x, g):
    return x * lax.rsqrt(jnp.mean(x * x, axis=-1, keepdims=True) + NORM_EPS) * g


def _split3(x):
    p1 = x.astype(BF16)
    r1 = x - p1.astype(F32)
    p2 = r1.astype(BF16)
    r2 = r1 - p2.astype(F32)
    return p1, p2, r2.astype(BF16)


def _dot3(pieces, m):
    return _dot(pieces[0], m) + _dot(pieces[1], m) + _dot(pieces[2], m)


def _full(shape):
    n = len(shape)
    return pl.BlockSpec(shape, lambda *_: (0,) * n)


def _tri(n, strict, upper):
    r = lax.broadcasted_iota(jnp.int32, (n, n), 0)
    c = lax.broadcasted_iota(jnp.int32, (n, n), 1)
    if upper:
        keep = (r < c) if strict else (r <= c)
    else:
        keep = (r > c) if strict else (r >= c)
    return jnp.where(keep, 1.0, 0.0).astype(BF16)


def _row_specs(rows, nb, tm):
    per = rows // nb // tm
    row = lambda width: pl.BlockSpec((tm, width), lambda i: (i, 0))
    fmaj = lambda width: pl.BlockSpec((1, width, tm), lambda i: (i // per, 0, i % per))
    return row, fmaj


def _rope_rows(u, cos, sa, sb):
    outs = []
    for c in range(u.shape[1] // LANES):
        blk = u[:, c * LANES:(c + 1) * LANES]
        outs.append(blk * cos + pltpu.roll(blk, LANES - ROT_HALF, 1) * sa + pltpu.roll(blk, ROT_HALF, 1) * sb)
    return jnp.concatenate(outs, axis=1)


def _rope_fmaj(ut, cos_t, sin_t, stride):
    assert ROT_HALF == SUBLANES
    outs = []
    for base in range(0, ut.shape[0], stride):
        x1 = ut[base:base + ROT_HALF, :]
        x2 = ut[base + ROT_HALF:base + ROT_DIM, :]
        outs += [x1 * cos_t - x2 * sin_t, x2 * cos_t + x1 * sin_t, ut[base + ROT_DIM:base + stride, :]]
    return jnp.concatenate(outs, axis=0)


def _store_q_aug(qta_ref, qt, bias_pieces):
    tm = qt.shape[1]
    sub = lax.broadcasted_iota(jnp.int32, (BIAS_SLOTS, tm), 0)
    ones = jnp.where((sub >= N_PIECES) & (sub < 2 * N_PIECES), 1.0, 0.0)
    for head in range(N_HEADS):
        parts = [qt[head * HEAD_DIM:(head + 1) * HEAD_DIM, :]]
        if bias_pieces is None:
            parts.append(jnp.zeros((QK_AUG - HEAD_DIM, tm), F32))
        else:
            g = head % GROUP
            grp = ones
            for n in reversed(range(N_PIECES)):
                grp = jnp.where(sub == n, bias_pieces[n][head:head + 1, :], grp)
            before, after = BIAS_SLOTS * g, QK_AUG - HEAD_DIM - BIAS_SLOTS * (g + 1)
            parts += ([jnp.zeros((before, tm), F32)] if before else []) + [grp]
            parts += [jnp.zeros((after, tm), F32)] if after else []
        qta_ref[0, head * QK_AUG:(head + 1) * QK_AUG, :] = jnp.concatenate(parts, axis=0).astype(BF16)


def _store_v_aug(vta_ref, vt):
    tm = vt.shape[1]
    pad = V_AUG - HEAD_DIM
    ones_row = jnp.where(lax.broadcasted_iota(jnp.int32, (pad, tm), 0) == 0, 1.0, 0.0)
    for j in range(N_KV):
        blk = jnp.concatenate([vt[j * HEAD_DIM:(j + 1) * HEAD_DIM, :], ones_row], axis=0)
        vta_ref[0, j * V_AUG:(j + 1) * V_AUG, :] = blk.astype(BF16)


def _pre_fox_prompt_kernel(x_ref, g_ref, wqt_ref, wk_ref, wkt_ref, wvt_ref, wf_ref, wft_ref, wz_ref,
                           bf_ref, bft_ref, low_ref, upp_ref, pk_ref, onek_ref,
                           qta_ref, ka_ref, vta_ref, kt_ref, vt_ref, lft_ref, gate_ref, ccarry, rcarry, *, per):
    @pl.when(pl.program_id(0) % per == 0)
    def _():
        ccarry[...] = jnp.zeros_like(ccarry)
        rcarry[...] = jnp.zeros_like(rcarry)

    tm = x_ref.shape[0]
    h = _rms_norm(x_ref[...], g_ref[...]).astype(BF16)
    lf = _log_sigmoid(_dot(h, wf_ref[...]) + bf_ref[...])
    lft = _log_sigmoid(_dot_nt(wft_ref[...], h) + bft_ref[...])
    lft_ref[0] = lft
    lp = _split3(lf)
    ccol = _dot(low_ref[...], lp[0]) + _dot(low_ref[...], lp[1]) + _dot(low_ref[...], lp[2]) + ccarry[...]
    crow = _dot3(_split3(lft), upp_ref[...]) + rcarry[...]
    ccarry[...] = ccol[tm - 1:tm, :]
    rcarry[...] = crow[:, tm - 1:tm]
    cp = _split3(ccol)
    ka = _dot(h, wk_ref[...]) + onek_ref[...]
    for n in range(N_PIECES):
        ka = ka + _dot(cp[n], pk_ref[n])
    ka_ref[...] = ka.astype(BF16)
    _store_q_aug(qta_ref, _dot_nt(wqt_ref[...], h), [p.astype(F32) for p in _split3(crow)])
    kt_ref[0] = _dot_nt(wkt_ref[...], h)
    vt = _dot_nt(wvt_ref[...], h)
    vt_ref[0] = vt
    _store_v_aug(vta_ref, vt)
    gate_ref[...] = _silu(_dot(h, wz_ref[...])).astype(BF16)


def _bias_placement():
    pk = np.zeros((N_PIECES, N_HEADS, K_AUG_LANES), np.float32)
    onek = np.zeros((1, K_AUG_LANES), np.float32)
    for head in range(N_HEADS):
        kvh, g = divmod(head, GROUP)
        slot = kvh * QK_AUG + HEAD_DIM + BIAS_SLOTS * g
        for n in range(N_PIECES):
            onek[0, slot + n] = 1.0
            pk[n, head, slot + N_PIECES + n] = -1.0
    return jnp.asarray(pk, BF16), jnp.asarray(onek)


def _attn_weights(wq, wk, wv):
    wk_pad = jnp.pad(wk.reshape(D_MODEL, N_KV, HEAD_DIM), ((0, 0), (0, 0), (0, QK_AUG - HEAD_DIM)))
    return dict(wqt=(wq * ATT_SCALE).T.astype(BF16), wk=wk_pad.reshape(D_MODEL, K_AUG_LANES).astype(BF16),
                wkt=wk.T.astype(BF16), wvt=wv.T.astype(BF16))


def _pre_fox_prompt(x, g, w, nb):
    rows = x.shape[0]
    seq = rows // nb
    tm = min(256, seq)
    per = seq // tm
    row, fmaj = _row_specs(rows, nb, tm)
    consts = [_tri(tm, False, False), _tri(tm, False, True), *_bias_placement()]
    names = ("wqt", "wk", "wkt", "wvt", "wf", "wft", "wz", "bf", "bft")
    return pl.pallas_call(
        functools.partial(_pre_fox_prompt_kernel, per=per),
        grid=(rows // tm,),
        in_specs=[row(D_MODEL), _full((1, D_MODEL))] + [_full(w[n].shape) for n in names]
                 + [_full(c.shape) for c in consts],
        out_specs=[fmaj(Q_AUG_ROWS), row(K_AUG_LANES), fmaj(N_KV * V_AUG), fmaj(KV_WIDTH), fmaj(KV_WIDTH),
                   fmaj(N_HEADS), row(D_MODEL)],
        out_shape=[jax.ShapeDtypeStruct((nb, Q_AUG_ROWS, seq), BF16), jax.ShapeDtypeStruct((rows, K_AUG_LANES), BF16),
                   jax.ShapeDtypeStruct((nb, N_KV * V_AUG, seq), BF16),
                   jax.ShapeDtypeStruct((nb, KV_WIDTH, seq), F32), jax.ShapeDtypeStruct((nb, KV_WIDTH, seq), F32),
                   jax.ShapeDtypeStruct((nb, N_HEADS, seq), F32), jax.ShapeDtypeStruct((rows, D_MODEL), BF16)],
        scratch_shapes=[pltpu.VMEM((1, N_HEADS), F32), pltpu.VMEM((N_HEADS, 1), F32)],
        compiler_params=_cparams("arbitrary"),
        name="pre_fox_prompt",
    )(x, g, *[w[n] for n in names], *consts)


def _pre_swa_prompt_kernel(x_ref, g_ref, wqt_ref, wk_ref, wkt_ref, wvt_ref, wz_ref,
                           cos_ref, sa_ref, sb_ref, cost_ref, sint_ref,
                           qta_ref, ka_ref, vta_ref, kt_ref, vt_ref, gate_ref):
    h = _rms_norm(x_ref[...], g_ref[...]).astype(BF16)
    cos_t, sin_t = cost_ref[...], sint_ref[...]
    _store_q_aug(qta_ref, _rope_fmaj(_dot_nt(wqt_ref[...], h), cos_t, sin_t, HEAD_DIM), None)
    ka_ref[...] = _rope_rows(_dot(h, wk_ref[...]), cos_ref[...], sa_ref[...], sb_ref[...]).astype(BF16)
    kt_ref[0] = _rope_fmaj(_dot_nt(wkt_ref[...], h), cos_t, sin_t, HEAD_DIM)
    vt = _dot_nt(wvt_ref[...], h)
    vt_ref[0] = vt
    _store_v_aug(vta_ref, vt)
    gate_ref[...] = _silu(_dot(h, wz_ref[...])).astype(BF16)


def _pre_swa_prompt(x, g, w, tables, nb):
    rows = x.shape[0]
    seq = rows // nb
    tm = min(256, seq)
    per = seq // tm
    row, fmaj = _row_specs(rows, nb, tm)
    tab = pl.BlockSpec((tm, LANES), lambda i: (i % per, 0))
    tab_t = pl.BlockSpec((ROT_HALF, tm), lambda i: (0, i % per))
    names = ("wqt", "wk", "wkt", "wvt", "wz")
    return pl.pallas_call(
        _pre_swa_prompt_kernel,
        grid=(rows // tm,),
        in_specs=[row(D_MODEL), _full((1, D_MODEL))] + [_full(w[n].shape) for n in names]
                 + [tab, tab, tab, tab_t, tab_t],
        out_specs=[fmaj(Q_AUG_ROWS), row(K_AUG_LANES), fmaj(N_KV * V_AUG), fmaj(KV_WIDTH), fmaj(KV_WIDTH),
                   row(D_MODEL)],
        out_shape=[jax.ShapeDtypeStruct((nb, Q_AUG_ROWS, seq), BF16), jax.ShapeDtypeStruct((rows, K_AUG_LANES), BF16),
                   jax.ShapeDtypeStruct((nb, N_KV * V_AUG, seq), BF16),
                   jax.ShapeDtypeStruct((nb, KV_WIDTH, seq), F32), jax.ShapeDtypeStruct((nb, KV_WIDTH, seq), F32),
                   jax.ShapeDtypeStruct((rows, D_MODEL), BF16)],
        compiler_params=_cparams("parallel"),
        name="pre_swa_prompt",
    )(x, g, *[w[n] for n in names], *tables)


def _pre_fox_decode_kernel(x_ref, g_ref, wq_ref, wkt_ref, wvt_ref, wft_ref, wz_ref, bft_ref,
                           q_ref, kt_ref, vt_ref, lft_ref, gate_ref):
    h = _rms_norm(x_ref[...], g_ref[...]).astype(BF16)
    q_ref[...] = _dot(h, wq_ref[...]).astype(BF16)
    kt_ref[0] = _dot_nt(wkt_ref[...], h)
    vt_ref[0] = _dot_nt(wvt_ref[...], h)
    lft_ref[0] = _log_sigmoid(_dot_nt(wft_ref[...], h) + bft_ref[...])
    gate_ref[...] = _silu(_dot(h, wz_ref[...])).astype(BF16)


def _pre_fox_decode(x, g, w):
    rows = x.shape[0]
    row, fmaj = _row_specs(rows, 1, rows)
    names = ("wq", "wkt", "wvt", "wft", "wz", "bft")
    return pl.pallas_call(
        _pre_fox_decode_kernel,
        grid=(1,),
        in_specs=[row(D_MODEL), _full((1, D_MODEL))] + [_full(w[n].shape) for n in names],
        out_specs=[row(D_MODEL), fmaj(KV_WIDTH), fmaj(KV_WIDTH), fmaj(N_HEADS), row(D_MODEL)],
        out_shape=[jax.ShapeDtypeStruct((rows, D_MODEL), BF16),
                   jax.ShapeDtypeStruct((1, KV_WIDTH, rows), F32), jax.ShapeDtypeStruct((1, KV_WIDTH, rows), F32),
                   jax.ShapeDtypeStruct((1, N_HEADS, rows), F32), jax.ShapeDtypeStruct((rows, D_MODEL), BF16)],
        compiler_params=_cparams("arbitrary"),
        name="pre_fox_decode",
    )(x, g, *[w[n] for n in names])


def _pre_swa_decode_kernel(x_ref, g_ref, wq_ref, wkt_ref, wvt_ref, wz_ref, cos_ref, sa_ref, sb_ref,
                           cost_ref, sint_ref, q_ref, kt_ref, vt_ref, gate_ref):
    h = _rms_norm(x_ref[...], g_ref[...]).astype(BF16)
    q_ref[...] = _rope_rows(_dot(h, wq_ref[...]), cos_ref[...], sa_ref[...], sb_ref[...]).astype(BF16)
    kt_ref[0] = _rope_fmaj(_dot_nt(wkt_ref[...], h), cost_ref[...], sint_ref[...], HEAD_DIM)
    vt_ref[0] = _dot_nt(wvt_ref[...], h)
    gate_ref[...] = _silu(_dot(h, wz_ref[...])).astype(BF16)


def _pre_swa_decode(x, g, w, tables):
    rows = x.shape[0]
    row, fmaj = _row_specs(rows, 1, rows)
    names = ("wq", "wkt", "wvt", "wz")
    return pl.pallas_call(
        _pre_swa_decode_kernel,
        grid=(1,),
        in_specs=[row(D_MODEL), _full((1, D_MODEL))] + [_full(w[n].shape) for n in names]
                 + [_full(t.shape) for t in tables],
        out_specs=[row(D_MODEL), fmaj(KV_WIDTH), fmaj(KV_WIDTH), row(D_MODEL)],
        out_shape=[jax.ShapeDtypeStruct((rows, D_MODEL), BF16),
                   jax.ShapeDtypeStruct((1, KV_WIDTH, rows), F32), jax.ShapeDtypeStruct((1, KV_WIDTH, rows), F32),
                   jax.ShapeDtypeStruct((rows, D_MODEL), BF16)],
        compiler_params=_cparams("arbitrary"),
        name="pre_swa_decode",
    )(x, g, *[w[n] for n in names], *tables)


def _pre_conv_kernel(x_ref, g_ref, wa_ref, wg_ref, wz_ref, glu_ref, gate_ref):
    h = _rms_norm(x_ref[...], g_ref[...]).astype(BF16)
    glu_ref[...] = _dot(h, wa_ref[...]) * _sigmoid(_dot(h, wg_ref[...]))
    gate_ref[...] = _silu(_dot(h, wz_ref[...])).astype(BF16)


def _pre_conv(x, g, w):
    rows = x.shape[0]
    tm = min(512, rows)
    row = pl.BlockSpec((tm, D_MODEL), lambda i: (i, 0))
    return pl.pallas_call(
        _pre_conv_kernel,
        grid=(rows // tm,),
        in_specs=[row, _full((1, D_MODEL)), _full(w["wa"].shape), _full(w["wg"].shape), _full(w["wz"].shape)],
        out_specs=[row, row],
        out_shape=[jax.ShapeDtypeStruct((rows, D_MODEL), F32), jax.ShapeDtypeStruct((rows, D_MODEL), BF16)],
        compiler_params=_cparams("parallel"),
        name="pre_conv",
    )(x, g, w["wa"], w["wg"], w["wz"])


def _post_mix_kernel(*refs, gated, final):
    refs = list(refs)
    a_ref = refs.pop(0)
    gate_ref = refs.pop(0) if gated else None
    x_ref, p_ref, wo_ref, ng_ref, wg_ref, bg_ref, wp_ref = refs[:7]
    fg_ref = refs[7] if final else None
    o_ref = refs[-1]
    a = a_ref[...]
    if gated:
        a = (a * gate_ref[...].astype(F32)).astype(BF16)
    x1 = x_ref[...] + _dot(a, wo_ref[...])
    h = _rms_norm(x1, ng_ref[...]).astype(BF16)
    gate2 = _sigmoid(_dot(h, wg_ref[...]) + bg_ref[...])
    x2 = x1 + gate2 * _dot(p_ref[...].astype(BF16), wp_ref[...])
    if final:
        x2 = _rms_norm(x2, fg_ref[...])
    o_ref[...] = x2


def _post_mix(a, gate, x, p_all, layer, w, final_g):
    rows = x.shape[0]
    tm = min(512, rows)
    row = lambda width: pl.BlockSpec((tm, width), lambda i: (i, 0))
    gated, final = gate is not None, final_g is not None
    pe = p_all.shape[2]
    args = [a] + ([gate] if gated else []) + [x, p_all, w["wo"], w["ng"], w["wg"], w["bg"], w["wp"]]
    specs = [row(D_MODEL)] * (2 if gated else 1)
    specs += [row(D_MODEL), pl.BlockSpec((None, tm, pe), lambda i: (layer, i, 0)),
              _full((D_MODEL, D_MODEL)), _full((1, D_MODEL)), _full((D_MODEL, D_MODEL)),
              _full((1, D_MODEL)), _full((pe, D_MODEL))]
    if final:
        args.append(final_g)
        specs.append(_full((1, D_MODEL)))
    return pl.pallas_call(
        functools.partial(_post_mix_kernel, gated=gated, final=final),
        grid=(rows // tm,),
        in_specs=specs,
        out_specs=row(D_MODEL),
        out_shape=jax.ShapeDtypeStruct((rows, D_MODEL), F32),
        compiler_params=_cparams("parallel"),
        name="post_mix",
    )(*args)


def _softmax_step(st, m, acc, vta):
    m_new = jnp.maximum(m, jnp.max(st, axis=0, keepdims=True))
    alpha = jnp.exp(m - m_new)
    p = jnp.exp(st - m_new).astype(BF16)
    return m_new, alpha * acc + _dot(vta, p)


def _finish_heads(accs, denoms, gate_ref, o_ref):
    outs = [(acc[0:HEAD_DIM, :] / den).T for acc, den in zip(accs, denoms)]
    o_ref[...] = (jnp.concatenate(outs, axis=1) * gate_ref[...].astype(F32)).astype(BF16)


def _fox_prompt_kernel(qta_ref, ka_ref, vta_ref, gate_ref, o_ref, *, tq):
    i = pl.program_id(2)
    half = tq // 2
    qts = [qta_ref[0, g * QK_AUG:(g + 1) * QK_AUG, :] for g in range(GROUP)]

    def step(j, carry):
        off = pl.multiple_of(j * tq, tq)
        ka = ka_ref[pl.ds(off, tq), :]
        vta = vta_ref[0, :, pl.ds(off, tq)]
        sts = [_dot(ka, qts[g]) for g in range(GROUP)]
        return tuple(_softmax_step(sts[g], carry[g][0], carry[g][1], vta) for g in range(GROUP))

    init = tuple((jnp.full((1, tq), -jnp.inf, F32), jnp.zeros((V_AUG, tq), F32)) for _ in range(GROUP))
    carry = lax.fori_loop(0, i, step, init)

    off = pl.multiple_of(i * tq, tq)
    ka = ka_ref[pl.ds(off, tq), :]
    vta = vta_ref[0, :, pl.ds(off, tq)]
    tri_a = (lax.broadcasted_iota(jnp.int32, (half, half), 0) <= lax.broadcasted_iota(jnp.int32, (half, half), 1))
    tri_b = (lax.broadcasted_iota(jnp.int32, (tq, half), 0)
             <= lax.broadcasted_iota(jnp.int32, (tq, half), 1) + half)
    sts_a = [jnp.where(tri_a, _dot(ka[0:half, :], qts[g][:, 0:half]), -jnp.inf) for g in range(GROUP)]
    sts_b = [jnp.where(tri_b, _dot(ka, qts[g][:, half:]), -jnp.inf) for g in range(GROUP)]
    accs = []
    for g in range(GROUP):
        m, acc = carry[g]
        _, acc_a = _softmax_step(sts_a[g], m[:, 0:half], acc[:, 0:half], vta[:, 0:half])
        _, acc_b = _softmax_step(sts_b[g], m[:, half:], acc[:, half:], vta)
        accs.append(jnp.concatenate([acc_a, acc_b], axis=1))
    _finish_heads(accs, [acc[HEAD_DIM:HEAD_DIM + 1, :] for acc in accs], gate_ref, o_ref)


def _fox_prompt(qta, ka, vta, gate):
    batch, _, seq = qta.shape
    rows = batch * seq
    tq = min(512, seq)
    assert seq % tq == 0 and tq % (2 * LANES) == 0
    nq = seq // tq
    width = GROUP * HEAD_DIM
    qspec = pl.BlockSpec((tq, width), lambda b, h, i: (b * nq + i, h))
    return pl.pallas_call(
        functools.partial(_fox_prompt_kernel, tq=tq),
        grid=(batch, N_KV, nq),
        in_specs=[pl.BlockSpec((1, GROUP * QK_AUG, tq), lambda b, h, i: (b, h, i)),
                  pl.BlockSpec((seq, QK_AUG), lambda b, h, i: (b, h)),
                  pl.BlockSpec((1, V_AUG, seq), lambda b, h, i: (b, h, 0)),
                  qspec],
        out_specs=qspec,
        out_shape=jax.ShapeDtypeStruct((rows, D_MODEL), BF16),
        compiler_params=_cparams("parallel", "parallel", "arbitrary"),
        name="fox_prompt",
    )(qta, ka, vta, gate)


SWA_QB = 4


def _swa_prompt_kernel(qta_ref, ka_ref, vta_ref, sink_ref, gate_ref, o_ref):
    n0 = pl.program_id(2) * SWA_QB
    ki = lax.broadcasted_iota(jnp.int32, (2 * Q_BLOCK, Q_BLOCK), 0)
    qi = lax.broadcasted_iota(jnp.int32, (2 * Q_BLOCK, Q_BLOCK), 1)
    starts, sts = [], []
    for r in range(SWA_QB):
        n = n0 + r
        start = pl.multiple_of(jnp.maximum(n - 1, 0) * Q_BLOCK, Q_BLOCK)
        rel = (n * Q_BLOCK + qi) - (start + ki)
        allowed = (rel >= 0) & (rel < WINDOW)
        ka = ka_ref[pl.ds(start, 2 * Q_BLOCK), :]
        starts.append(start)
        sts.append([jnp.where(allowed, _dot(ka, qta_ref[0, g * QK_AUG:(g + 1) * QK_AUG,
                                                        r * Q_BLOCK:(r + 1) * Q_BLOCK]), -jnp.inf)
                    for g in range(GROUP)])
    sinks = [sink_ref[0, :, g:g + 1] for g in range(GROUP)]
    ms = [[jnp.maximum(jnp.max(sts[r][g], axis=0, keepdims=True), sinks[g]) for g in range(GROUP)]
          for r in range(SWA_QB)]
    ps = [[jnp.exp(sts[r][g] - ms[r][g]).astype(BF16) for g in range(GROUP)] for r in range(SWA_QB)]
    for r in range(SWA_QB):
        vta = vta_ref[0, :, pl.ds(starts[r], 2 * Q_BLOCK)]
        accs = [_dot(vta, ps[r][g]) for g in range(GROUP)]
        denoms = [accs[g][HEAD_DIM:HEAD_DIM + 1, :] + jnp.exp(sinks[g] - ms[r][g]) for g in range(GROUP)]
        rows = slice(r * Q_BLOCK, (r + 1) * Q_BLOCK)
        _finish_heads(accs, denoms, gate_ref.at[rows, :], o_ref.at[rows, :])


def _swa_prompt(qta, ka, vta, sinks, gate):
    batch, _, seq = qta.shape
    rows = batch * seq
    tq = SWA_QB * Q_BLOCK
    nq = seq // tq
    width = GROUP * HEAD_DIM
    qspec = pl.BlockSpec((tq, width), lambda b, h, i: (b * nq + i, h))
    return pl.pallas_call(
        _swa_prompt_kernel,
        grid=(batch, N_KV, nq),
        in_specs=[pl.BlockSpec((1, GROUP * QK_AUG, tq), lambda b, h, i: (b, h, i)),
                  pl.BlockSpec((seq, QK_AUG), lambda b, h, i: (b, h)),
                  pl.BlockSpec((1, V_AUG, seq), lambda b, h, i: (b, h, 0)),
                  pl.BlockSpec((1, 1, GROUP), lambda b, h, i: (h, 0, 0)),
                  qspec],
        out_specs=qspec,
        out_shape=jax.ShapeDtypeStruct((rows, D_MODEL), BF16),
        compiler_params=_cparams("parallel", "parallel", "arbitrary"),
        name="swa_prompt",
    )(qta, ka, vta, sinks.reshape(N_KV, 1, GROUP), gate)


def _block_diag_mask():
    r = lax.broadcasted_iota(jnp.int32, (DEC_ROWS, KV_WIDTH), 0)
    c = lax.broadcasted_iota(jnp.int32, (DEC_ROWS, KV_WIDTH), 1)
    return ((r % N_HEADS) // GROUP) == (c // HEAD_DIM)


def _fold_heads(o_full):
    om = jnp.where(_block_diag_mask(), o_full, 0.0)
    return (om[:, 0:HEAD_DIM] + om[:, HEAD_DIM:2 * HEAD_DIM]
            + om[:, 2 * HEAD_DIM:3 * HEAD_DIM] + om[:, 3 * HEAD_DIM:4 * HEAD_DIM])


def _new_token_mask():
    lane = lax.broadcasted_iota(jnp.int32, (DEC_ROWS, PAGE), 1)
    t = lax.broadcasted_iota(jnp.int32, (DEC_ROWS, PAGE), 0) // N_HEADS
    return (lane >= NEW_LANE) & (lane - NEW_LANE <= t)


def _fox_decode_kernel(pt_ref, qbd_ref, knew_ref, vnew_ref, lfnew_ref, kc_ref, vc_ref, lc_ref, o_ref,
                       kbuf, vbuf, lbuf, sems, *, layer, cp):
    b = pl.program_id(0)
    nb = pl.num_programs(0)
    nch = pt_ref.shape[1] // cp

    def copies(seq, ci, slot):
        base = (nch - 1 - ci) * cp
        out = []
        for p in range(cp):
            page = pt_ref[seq, base + p]
            out.append(pltpu.make_async_copy(kc_ref.at[layer, page], kbuf.at[slot, p], sems.at[0, slot]))
            out.append(pltpu.make_async_copy(vc_ref.at[layer, page], vbuf.at[slot, p], sems.at[1, slot]))
            out.append(pltpu.make_async_copy(lc_ref.at[layer, page], lbuf.at[slot, p], sems.at[2, slot]))
        return out

    total = nb * nch

    def start_chunk(g):
        for c in copies(g // nch, g % nch, g % DEC_SLOTS):
            c.start()

    @pl.when(b == 0)
    def _():
        for g in range(DEC_SLOTS - 1):
            start_chunk(g)

    qbd = qbd_ref[0]

    x = lfnew_ref[0]
    cn = x + pltpu.roll(x, 1, 1) + pltpu.roll(x, 2, 1) + pltpu.roll(x, 3, 1)
    s_new = _dot(qbd, knew_ref[0].astype(BF16))
    cqs = [cn[:, NEW_LANE + t:NEW_LANE + t + 1] for t in range(DEC_T)]
    s_new = s_new + jnp.concatenate([cqs[t] - cn for t in range(DEC_T)], axis=0)
    s_new = jnp.where(_new_token_mask(), s_new, -jnp.inf)
    cq = jnp.concatenate(cqs, axis=0)
    m = jnp.max(s_new, axis=-1, keepdims=True)
    p = jnp.exp(s_new - m)
    l = jnp.sum(p, axis=-1, keepdims=True)
    acc = _dot_nt(p.astype(BF16), vnew_ref[0].astype(BF16))

    sfx = jnp.concatenate([_tri(PAGE, strict=True, upper=False), jnp.ones((PAGE, LANES), BF16)], axis=1)

    def chunk(ci, carry):
        m, l, acc, later = carry
        g = b * nch + ci
        slot = g % DEC_SLOTS
        for c in copies(b, ci, slot):
            c.wait()

        @pl.when(g + DEC_SLOTS - 1 < total)
        def _():
            start_chunk(g + DEC_SLOTS - 1)

        st = _dot3(_split3(lbuf[slot].reshape(cp * N_HEADS, PAGE)), sfx).reshape(cp, N_HEADS, 2 * LANES)
        bias = [None] * cp
        for pg in reversed(range(cp)):
            bias[pg] = jnp.concatenate([st[pg, :, :LANES] + later] * DEC_T, axis=0) + cq
            later = later + st[pg, :, LANES:]
        ss = [_dot(qbd, kbuf[slot, pg].astype(BF16)) + bias[pg] for pg in range(cp)]
        per = cp // DEC_GROUPS if cp % DEC_GROUPS == 0 else cp
        parts = []
        for lo in range(0, cp, per):
            grp = ss[lo:lo + per]
            m_loc = jnp.max(functools.reduce(jnp.maximum, grp), axis=-1, keepdims=True)
            ps = [jnp.exp(s - m_loc) for s in grp]
            l_loc = jnp.sum(functools.reduce(jnp.add, ps), axis=-1, keepdims=True)
            parts.append((m_loc, l_loc, lo, ps))
        accs = []
        for m_loc, l_loc, lo, ps in parts:
            a = _dot_nt(ps[0].astype(BF16), vbuf[slot, lo].astype(BF16))
            for n in range(1, len(ps)):
                a = a + _dot_nt(ps[n].astype(BF16), vbuf[slot, lo + n].astype(BF16))
            accs.append(a)
        m_new = functools.reduce(jnp.maximum, [m] + [pt[0] for pt in parts])
        alpha = jnp.exp(m - m_new)
        l, acc = alpha * l, alpha * acc
        for (m_loc, l_loc, _, _), a in zip(parts, accs):
            w = jnp.exp(m_loc - m_new)
            l, acc = l + w * l_loc, acc + w * a
        return m_new, l, acc, later

    m, l, acc, _ = lax.fori_loop(0, nch, chunk, (m, l, acc, jnp.zeros((N_HEADS, LANES), F32)))
    o_ref[0] = _fold_heads(acc / l)


def _fox_decode(page_table, qbd, knew, vnew, lfnew, cache_kt, cache_vt, cache_lft, layer):
    batch, n_pages = page_table.shape
    cp = 16
    while n_pages % cp:
        cp //= 2
    assert batch * (n_pages // cp) >= DEC_SLOTS - 1
    seq3 = lambda r, c: pl.BlockSpec((1, r, c), lambda b, pt: (b, 0, 0))
    grid_spec = pltpu.PrefetchScalarGridSpec(
        num_scalar_prefetch=1,
        grid=(batch,),
        in_specs=[seq3(DEC_ROWS, KV_WIDTH), seq3(KV_WIDTH, PAGE), seq3(KV_WIDTH, PAGE), seq3(N_HEADS, PAGE),
                  pl.BlockSpec(memory_space=pl.ANY), pl.BlockSpec(memory_space=pl.ANY),
                  pl.BlockSpec(memory_space=pl.ANY)],
        out_specs=seq3(DEC_ROWS, HEAD_DIM),
        scratch_shapes=[pltpu.VMEM((DEC_SLOTS, cp, KV_WIDTH, PAGE), F32),
                        pltpu.VMEM((DEC_SLOTS, cp, KV_WIDTH, PAGE), F32),
                        pltpu.VMEM((DEC_SLOTS, cp, N_HEADS, PAGE), F32), pltpu.SemaphoreType.DMA((3, DEC_SLOTS))],
    )
    return pl.pallas_call(
        functools.partial(_fox_decode_kernel, layer=layer, cp=cp),
        grid_spec=grid_spec,
        out_shape=jax.ShapeDtypeStruct((batch, DEC_ROWS, HEAD_DIM), F32),
        compiler_params=_cparams("arbitrary"),
        name="fox_decode",
    )(page_table, qbd, knew, vnew, lfnew, cache_kt, cache_vt, cache_lft)


def _swa_decode_kernel(qbd_ref, knew_ref, vnew_ref, sk_ref, sv_ref, sink_ref, o_ref, nk_ref, nv_ref):
    qbd = qbd_ref[0]
    sk, sv, kn, vn = sk_ref[0, 0], sv_ref[0, 0], knew_ref[0], vnew_ref[0]
    lane = lax.broadcasted_iota(jnp.int32, (DEC_ROWS, PAGE), 1)
    t = lax.broadcasted_iota(jnp.int32, (DEC_ROWS, PAGE), 0) // N_HEADS
    s_old = jnp.where(lane > t, _dot(qbd, sk.astype(BF16)), -jnp.inf)
    s_new = jnp.where(_new_token_mask(), _dot(qbd, kn.astype(BF16)), -jnp.inf)
    sink = sink_ref[...]
    m = jnp.maximum(jnp.maximum(jnp.max(s_old, axis=-1, keepdims=True), jnp.max(s_new, axis=-1, keepdims=True)),
                    sink)
    p_old = jnp.exp(s_old - m)
    p_new = jnp.exp(s_new - m)
    denom = (jnp.sum(p_old, axis=-1, keepdims=True) + jnp.sum(p_new, axis=-1, keepdims=True)
             + jnp.exp(sink - m))
    o_full = _dot_nt(p_old.astype(BF16), sv.astype(BF16)) + _dot_nt(p_new.astype(BF16), vn.astype(BF16))
    o_ref[0] = _fold_heads(o_full / denom)
    keep_new = lax.broadcasted_iota(jnp.int32, (KV_WIDTH, PAGE), 1) >= NEW_LANE
    nk_ref[0] = jnp.where(keep_new, kn, pltpu.roll(sk, NEW_LANE, 1))
    nv_ref[0] = jnp.where(keep_new, vn, pltpu.roll(sv, NEW_LANE, 1))


def _swa_decode(qbd, knew, vnew, state_kt, state_vt, sink_rows, layer):
    batch, buf = state_kt.shape[1], state_kt.shape[3]
    assert buf == WINDOW == PAGE
    seq3 = lambda r, c: pl.BlockSpec((1, r, c), lambda b: (b, 0, 0))
    st = pl.BlockSpec((1, 1, KV_WIDTH, buf), lambda b: (layer, b, 0, 0))
    return pl.pallas_call(
        _swa_decode_kernel,
        grid=(batch,),
        in_specs=[seq3(DEC_ROWS, KV_WIDTH), seq3(KV_WIDTH, PAGE), seq3(KV_WIDTH, PAGE), st, st,
                  _full((DEC_ROWS, 1))],
        out_specs=[seq3(DEC_ROWS, HEAD_DIM), seq3(KV_WIDTH, buf), seq3(KV_WIDTH, buf)],
        out_shape=[jax.ShapeDtypeStruct((batch, DEC_ROWS, HEAD_DIM), F32),
                   jax.ShapeDtypeStruct((batch, KV_WIDTH, buf), F32),
                   jax.ShapeDtypeStruct((batch, KV_WIDTH, buf), F32)],
        compiler_params=_cparams("parallel"),
        name="swa_decode",
    )(qbd, knew, vnew, state_kt, state_vt, sink_rows)


HALO = 32
CONV_CHUNK = 32


def _conv_tail(y, gate, lng, lnb):
    yc = y - jnp.mean(y, axis=-1, keepdims=True)
    yn = yc * lax.rsqrt(jnp.mean(yc * yc, axis=-1, keepdims=True) + NORM_EPS) * lng + lnb
    return (_silu(yn) * gate.astype(F32)).astype(BF16)


def _conv_prompt_kernel(x_ref, g_ref, wa_ref, wg_ref, wz_ref, dw_ref, db_ref, lng_ref, lnb_ref,
                        a_ref, st_ref, buf, shifted, gbuf):
    i = pl.program_id(1)
    tm = x_ref.shape[0]
    h = _rms_norm(x_ref[...], g_ref[...]).astype(BF16)
    glu = _dot(h, wa_ref[...]) * _sigmoid(_dot(h, wg_ref[...]))
    gate = _silu(_dot(h, wz_ref[...]))

    @pl.when(i == 0)
    def _():
        buf[0:HALO, :] = jnp.zeros((HALO, D_MODEL), F32)

    buf[HALO:HALO + tm, :] = glu
    gbuf[...] = gate
    first = HALO - CONV_HIST
    for r in range(SUBLANES):
        span = tm + SUBLANES * ((CONV_HIST - r) // SUBLANES)
        shifted[r, 0:span, :] = buf[first + r:first + r + span, :]

    def chunk(c, _):
        off = pl.multiple_of(c * CONV_CHUNK, CONV_CHUNK)
        y = jnp.zeros((CONV_CHUNK, D_MODEL), F32) + db_ref[...]
        for w in range(CONV_WIDTH):
            start = off + SUBLANES * (w // SUBLANES)
            y = y + shifted[w % SUBLANES, pl.ds(start, CONV_CHUNK), :] * dw_ref[w:w + 1, :]
        rows = pl.ds(off, CONV_CHUNK)
        a_ref[rows, :] = _conv_tail(y, gbuf[rows, :], lng_ref[...], lnb_ref[...])
        return 0

    lax.fori_loop(0, tm // CONV_CHUNK, chunk, 0)

    @pl.when(i == pl.num_programs(1) - 1)
    def _():
        st_ref[0] = buf[HALO + tm - CONV_HIST:HALO + tm, :]

    buf[0:HALO, :] = buf[tm:tm + HALO, :]


def _conv_prompt(x, g, w, batch, seq):
    rows = batch * seq
    tm = min(256, seq)
    nt = seq // tm
    row = pl.BlockSpec((tm, D_MODEL), lambda b, i: (b * nt + i, 0))
    return pl.pallas_call(
        _conv_prompt_kernel,
        grid=(batch, nt),
        in_specs=[row, _full((1, D_MODEL)), _full(w["wa"].shape), _full(w["wg"].shape), _full(w["wz"].shape),
                  _full((CONV_WIDTH, D_MODEL)), _full((1, D_MODEL)), _full((1, D_MODEL)), _full((1, D_MODEL))],
        out_specs=[row, pl.BlockSpec((1, CONV_HIST, D_MODEL), lambda b, i: (b, 0, 0))],
        out_shape=[jax.ShapeDtypeStruct((rows, D_MODEL), BF16),
                   jax.ShapeDtypeStruct((batch, CONV_HIST, D_MODEL), F32)],
        scratch_shapes=[pltpu.VMEM((HALO + tm, D_MODEL), F32),
                        pltpu.VMEM((SUBLANES, tm + SUBLANES * (CONV_HIST // SUBLANES), D_MODEL), F32),
                        pltpu.VMEM((tm, D_MODEL), F32)],
        compiler_params=_cparams("arbitrary", "arbitrary"),
        name="conv_prompt",
    )(x, g, w["wa"], w["wg"], w["wz"], w["dw"], w["db"], w["lng"], w["lnb"])


def _conv_decode_kernel(glu_ref, gate_ref, st_ref, dw_ref, db_ref, lng_ref, lnb_ref, a_ref, ns_ref, pad):
    sb = glu_ref.shape[1]
    pad[0:CONV_HIST] = st_ref[0]
    pad[CONV_HIST:CONV_HIST + DEC_T] = glu_ref[...]
    y = jnp.zeros((DEC_T, sb, D_MODEL), F32) + db_ref[...]
    for w in range(CONV_WIDTH):
        y = y + pad[w:w + DEC_T] * dw_ref[w:w + 1, :]
    a_ref[...] = _conv_tail(y, gate_ref[...], lng_ref[...], lnb_ref[...])
    ns_ref[...] = pad[DEC_T:DEC_T + CONV_HIST]


def _conv_decode(glu, gate, state, w, layer):
    batch = state.shape[2]
    sb = 16
    while batch % sb:
        sb //= 2
    tok = pl.BlockSpec((DEC_T, sb, D_MODEL), lambda i: (0, i, 0))
    return pl.pallas_call(
        _conv_decode_kernel,
        grid=(batch // sb,),
        in_specs=[tok, tok, pl.BlockSpec((1, CONV_HIST, sb, D_MODEL), lambda i: (layer, 0, i, 0)),
                  _full((CONV_WIDTH, D_MODEL)), _full((1, D_MODEL)), _full((1, D_MODEL)), _full((1, D_MODEL))],
        out_specs=[tok, pl.BlockSpec((CONV_HIST, sb, D_MODEL), lambda i: (0, i, 0))],
        out_shape=[jax.ShapeDtypeStruct((DEC_T, batch, D_MODEL), BF16),
                   jax.ShapeDtypeStruct((CONV_HIST, batch, D_MODEL), F32)],
        scratch_shapes=[pltpu.VMEM((CONV_HIST + DEC_T, sb, D_MODEL), F32)],
        compiler_params=_cparams("parallel"),
        name="conv_decode",
    )(glu.reshape(DEC_T, batch, D_MODEL), gate.reshape(DEC_T, batch, D_MODEL), state,
      w["dw"], w["db"], w["lng"], w["lnb"])


def _rope_tables(pos):
    inv_freq = ROPE_THETA ** (-jnp.arange(ROT_HALF, dtype=F32) / ROT_HALF)
    ang = pos.astype(F32)[:, None] * inv_freq[None, :]
    cos, sin = jnp.cos(ang), jnp.sin(ang)
    n = pos.shape[0]
    rest = jnp.zeros((n, HEAD_DIM - ROT_DIM), F32)
    zero = jnp.zeros((n, ROT_HALF), F32)
    cos_h = jnp.concatenate([cos, cos, rest + 1.0], axis=1)
    sa_h = jnp.concatenate([-sin, zero, rest], axis=1)
    sb_h = jnp.concatenate([zero, sin, rest], axis=1)
    rep = LANES // HEAD_DIM
    return tuple(jnp.tile(t, (1, rep)) for t in (cos_h, sa_h, sb_h)) + (cos.T, sin.T)


def _decode_queries(q_rows, batch):
    q4 = q_rows.reshape(DEC_T, batch, N_HEADS, 1, HEAD_DIM).transpose(1, 0, 2, 3, 4)
    own = (jnp.arange(N_HEADS)[:, None] // GROUP) == jnp.arange(N_KV)[None, :]
    return jnp.where(own[None, None, :, :, None], q4, jnp.zeros((), q_rows.dtype)).reshape(batch, DEC_ROWS, KV_WIDTH)


def _new_token_pages(xt, batch):
    width = xt.shape[1]
    per_seq = xt.reshape(width, DEC_T, batch).transpose(2, 0, 1)
    return jnp.pad(per_seq, ((0, 0), (0, 0), (NEW_LANE, 0)))


def _decode_out_rows(o, batch):
    return o.reshape(batch, DEC_T, D_MODEL).transpose(1, 0, 2).reshape(DEC_T * batch, D_MODEL)


def _sample_major(xt, batch, inner):
    return xt.reshape(inner + (DEC_T, batch)).transpose((len(inner) + 1, len(inner)) + tuple(range(len(inner))))


def kernel(x_prompt, x_sample, cache_fox_k, cache_fox_v, cache_fox_lf, page_table, state_swa_k, state_swa_v,
           state_conv, p_prompt, p_sample, norm_g, fox_w_in, fox_b_f, fox_w_out, swa_w_in, swa_sinks, swa_w_out,
           conv_w_in, conv_dw_w, conv_dw_b, conv_ln_g, conv_ln_b, conv_w_out, pe_w_proj, pe_norm_g, pe_w_gate,
           pe_b_gate, final_norm_g):
    batch, seq, _ = x_prompt.shape
    dec_batch, dec_t, _ = x_sample.shape
    assert dec_t == DEC_T and seq % (SWA_QB * Q_BLOCK) == 0
    depth = norm_g.shape[0]
    past_len = page_table.shape[1] * PAGE
    buf = state_swa_k.shape[2]
    rows_p, rows_s = batch * seq, dec_batch * dec_t
    att = N_HEADS * HEAD_DIM

    xp = x_prompt.reshape(rows_p, D_MODEL)
    xs = x_sample.transpose(1, 0, 2).reshape(rows_s, D_MODEL)
    pp_all = p_prompt.reshape(depth, rows_p, -1)
    ps_all = p_sample.transpose(0, 2, 1, 3).reshape(depth, rows_s, -1)
    fmaj = lambda c: c.transpose(0, 1, 3, 4, 2).reshape(c.shape[:2] + (KV_WIDTH, c.shape[2]))
    cache_kt, cache_vt = fmaj(cache_fox_k), fmaj(cache_fox_v)
    cache_lft = cache_fox_lf.transpose(0, 1, 3, 2)
    st_kt, st_vt = fmaj(state_swa_k), fmaj(state_swa_v)
    st_conv = state_conv.transpose(0, 2, 1, 3)
    heads_major = lambda t: t.reshape(t.shape[0], N_KV, HEAD_DIM, t.shape[2]).transpose(0, 3, 1, 2)

    fox_k_p, fox_v_p, fox_lf_p, fox_k_s, fox_v_s, fox_lf_s = [], [], [], [], [], []
    swa_k_p, swa_v_p, swa_k_s, swa_v_s = [], [], [], []
    conv_p, conv_s = [], []

    for i in range(depth):
        kind, j = i % N_MIXERS, i // N_MIXERS
        g = norm_g[i].reshape(1, D_MODEL)
        if kind == 0:
            w_in = fox_w_in[j]
            wq, wk = w_in[:, :att], w_in[:, att:att + KV_WIDTH]
            wv = w_in[:, att + KV_WIDTH:att + 2 * KV_WIDTH]
            wf = w_in[:, att + 2 * KV_WIDTH:att + 2 * KV_WIDTH + N_HEADS]
            w = dict(_attn_weights(wq, wk, wv), wq=(wq * ATT_SCALE).astype(BF16),
                     wf=wf.astype(BF16), wft=wf.T.astype(BF16),
                     wz=w_in[:, att + 2 * KV_WIDTH + N_HEADS:].astype(BF16),
                     bf=fox_b_f[j].reshape(1, N_HEADS), bft=fox_b_f[j].reshape(N_HEADS, 1))
            w_out = fox_w_out[j]
            qta, ka, vta, kt, vt, lft, gate = _pre_fox_prompt(xp, g, w, batch)
            ap = _fox_prompt(qta, ka, vta, gate)
            gate_p = None
            fox_k_p.append(heads_major(kt))
            fox_v_p.append(heads_major(vt))
            fox_lf_p.append(lft.transpose(0, 2, 1))
            q, kt, vt, lft, gate_s = _pre_fox_decode(xs, g, w)
            o = _fox_decode(page_table, _decode_queries(q, dec_batch), _new_token_pages(kt, dec_batch),
                            _new_token_pages(vt, dec_batch), _new_token_pages(lft, dec_batch),
                            cache_kt, cache_vt, cache_lft, j)
            a_s = _decode_out_rows(o, dec_batch)
            fox_k_s.append(_sample_major(kt, dec_batch, (N_KV, HEAD_DIM)))
            fox_v_s.append(_sample_major(vt, dec_batch, (N_KV, HEAD_DIM)))
            fox_lf_s.append(_sample_major(lft, dec_batch, (N_HEADS,)))
        elif kind == 1:
            w_in = swa_w_in[j]
            wq, wk = w_in[:, :att], w_in[:, att:att + KV_WIDTH]
            wv = w_in[:, att + KV_WIDTH:att + 2 * KV_WIDTH]
            w = dict(_attn_weights(wq, wk, wv), wq=(wq * ATT_SCALE).astype(BF16),
                     wz=w_in[:, att + 2 * KV_WIDTH:].astype(BF16))
            w_out = swa_w_out[j]
            tables = _rope_tables(jnp.arange(seq, dtype=jnp.int32))
            qta, ka, vta, kt, vt, gate = _pre_swa_prompt(xp, g, w, tables, batch)
            ap = _swa_prompt(qta, ka, vta, swa_sinks[j], gate)
            gate_p = None
            swa_k_p.append(heads_major(kt[:, :, seq - buf:]))
            swa_v_p.append(heads_major(vt[:, :, seq - buf:]))
            pos_s = past_len + jnp.repeat(jnp.arange(DEC_T, dtype=jnp.int32), dec_batch)
            q, kt, vt, gate_s = _pre_swa_decode(xs, g, w, _rope_tables(pos_s))
            sink_rows = jnp.tile(swa_sinks[j].astype(F32), DEC_T).reshape(DEC_ROWS, 1)
            o, nkt, nvt = _swa_decode(_decode_queries(q, dec_batch), _new_token_pages(kt, dec_batch),
                                      _new_token_pages(vt, dec_batch), st_kt, st_vt, sink_rows, j)
            a_s = _decode_out_rows(o, dec_batch)
            swa_k_s.append(heads_major(nkt))
            swa_v_s.append(heads_major(nvt))
        else:
            w_in = conv_w_in[j]
            cd = w_in.shape[1] // 3
            w = dict(wa=w_in[:, :cd].astype(BF16), wg=w_in[:, cd:2 * cd].astype(BF16),
                     wz=w_in[:, 2 * cd:].astype(BF16), dw=conv_dw_w[j], db=conv_dw_b[j].reshape(1, cd),
                     lng=conv_ln_g[j].reshape(1, cd), lnb=conv_ln_b[j].reshape(1, cd))
            w_out = conv_w_out[j]
            ap, st = _conv_prompt(xp, g, w, batch, seq)
            gate_p = None
            conv_p.append(st)
            glu, gate = _pre_conv(xs, g, w)
            a_s, ns = _conv_decode(glu, gate, st_conv, w, j)
            a_s = a_s.reshape(rows_s, D_MODEL)
            gate_s = None
            conv_s.append(ns.transpose(1, 0, 2))
        wpost = dict(wo=w_out.astype(BF16), ng=pe_norm_g[i].reshape(1, D_MODEL), wg=pe_w_gate[i].astype(BF16),
                     bg=pe_b_gate[i].reshape(1, D_MODEL), wp=pe_w_proj[i].astype(BF16))
        fg = final_norm_g.reshape(1, D_MODEL) if i == depth - 1 else None
        xp = _post_mix(ap, gate_p, xp, pp_all, i, wpost, fg)
        xs = _post_mix(a_s, gate_s, xs, ps_all, i, wpost, fg)

    return (xp.reshape(batch, seq, D_MODEL), xs.reshape(DEC_T, dec_batch, D_MODEL).transpose(1, 0, 2),
            jnp.stack(fox_k_p), jnp.stack(fox_v_p), jnp.stack(fox_lf_p),
            jnp.stack(fox_k_s), jnp.stack(fox_v_s), jnp.stack(fox_lf_s),
            jnp.stack(swa_k_p), jnp.stack(swa_v_p), jnp.stack(swa_k_s), jnp.stack(swa_v_s),
            jnp.stack(conv_p), jnp.stack(conv_s))
```

```python
import functools

import numpy as np

import jax
import jax.numpy as jnp
from jax import lax
from jax.experimental import pallas as pl
from jax.experimental.pallas import tpu as pltpu

F32 = jnp.float32
BF16 = jnp.bfloat16

D_MODEL = 1024
HEAD_DIM = 64
N_HEADS = 16
N_KV = 4
GROUP = 4
KV_WIDTH = N_KV * HEAD_DIM
ATT_SCALE = HEAD_DIM ** -0.5
N_MIXERS = 3
Q_BLOCK = 128
WINDOW = 128
ROPE_THETA = 500000.0
ROT_DIM = HEAD_DIM // 4
ROT_HALF = ROT_DIM // 2
CONV_WIDTH = 31
CONV_HIST = CONV_WIDTH - 1
NORM_EPS = 1e-6
PAGE = 128
LANES = 128
SUBLANES = 8
VMEM_LIMIT = 52 * 1024 * 1024
DEC_T = 4
DEC_ROWS = DEC_T * N_HEADS
NEW_LANE = PAGE - DEC_T
DEC_SLOTS = 3
DEC_GROUPS = 4

QK_AUG = 128
BIAS_SLOTS = 8
N_PIECES = 3
V_AUG = 80
Q_AUG_ROWS = N_HEADS * QK_AUG
K_AUG_LANES = N_KV * QK_AUG

NT_DIMS = (((1,), (1,)), ((), ()))


def _cparams(*sem, flags=None):
    return pltpu.CompilerParams(dimension_semantics=sem, vmem_limit_bytes=VMEM_LIMIT, flags=flags)


def _dot(a, b):
    return jnp.dot(a, b, preferred_element_type=F32)


def _dot_nt(a, b):
    return lax.dot_general(a, b, NT_DIMS, preferred_element_type=F32)


def _sigmoid(x):
    return 1.0 / (1.0 + jnp.exp(-x))


def _silu(x):
    return x * _sigmoid(x)


def _log_sigmoid(x):
    return jnp.minimum(x, 0.0) - jnp.log1p(jnp.exp(-jnp.abs(x)))


def _rms_norm(x, g):
    return x * lax.rsqrt(jnp.mean(x * x, axis=-1, keepdims=True) + NORM_EPS) * g


def _split3(x):
    p1 = x.astype(BF16)
    r1 = x - p1.astype(F32)
    p2 = r1.astype(BF16)
    r2 = r1 - p2.astype(F32)
    return p1, p2, r2.astype(BF16)


def _dot3(pieces, m):
    return _dot(pieces[0], m) + _dot(pieces[1], m) + _dot(pieces[2], m)


def _full(shape):
    n = len(shape)
    return pl.BlockSpec(shape, lambda *_: (0,) * n)


def _tri(n, strict, upper):
    r = lax.broadcasted_iota(jnp.int32, (n, n), 0)
    c = lax.broadcasted_iota(jnp.int32, (n, n), 1)
    if upper:
        keep = (r < c) if strict else (r <= c)
    else:
        keep = (r > c) if strict else (r >= c)
    return jnp.where(keep, 1.0, 0.0).astype(BF16)


def _row_specs(rows, nb, tm):
    per = rows // nb // tm
    row = lambda width: pl.BlockSpec((tm, width), lambda i: (i, 0))
    fmaj = lambda width: pl.BlockSpec((1, width, tm), lambda i: (i // per, 0, i % per))
    return row, fmaj


def _rope_rows(u, cos, sa, sb):
    outs = []
    for c in range(u.shape[1] // LANES):
        blk = u[:, c * LANES:(c + 1) * LANES]
        outs.append(blk * cos + pltpu.roll(blk, LANES - ROT_HALF, 1) * sa + pltpu.roll(blk, ROT_HALF, 1) * sb)
    return jnp.concatenate(outs, axis=1)


def _rope_fmaj(ut, cos_t, sin_t, stride):
    assert ROT_HALF == SUBLANES
    outs = []
    for base in range(0, ut.shape[0], stride):
        x1 = ut[base:base + ROT_HALF, :]
        x2 = ut[base + ROT_HALF:base + ROT_DIM, :]
        outs += [x1 * cos_t - x2 * sin_t, x2 * cos_t + x1 * sin_t, ut[base + ROT_DIM:base + stride, :]]
    return jnp.concatenate(outs, axis=0)


def _store_q_aug(qta_ref, qt, bias_pieces):
    tm = qt.shape[1]
    sub = lax.broadcasted_iota(jnp.int32, (BIAS_SLOTS, tm), 0)
    ones = jnp.where((sub >= N_PIECES) & (sub < 2 * N_PIECES), 1.0, 0.0)
    for head in range(N_HEADS):
        parts = [qt[head * HEAD_DIM:(head + 1) * HEAD_DIM, :]]
        if bias_pieces is None:
            parts.append(jnp.zeros((QK_AUG - HEAD_DIM, tm), F32))
        else:
            g = head % GROUP
            grp = ones
            for n in reversed(range(N_PIECES)):
                grp = jnp.where(sub == n, bias_pieces[n][head:head + 1, :], grp)
            before, after = BIAS_SLOTS * g, QK_AUG - HEAD_DIM - BIAS_SLOTS * (g + 1)
            parts += ([jnp.zeros((before, tm), F32)] if before else []) + [grp]
            parts += [jnp.zeros((after, tm), F32)] if after else []
        qta_ref[0, head * QK_AUG:(head + 1) * QK_AUG, :] = jnp.concatenate(parts, axis=0).astype(BF16)


def _store_v_aug(vta_ref, vt):
    tm = vt.shape[1]
    pad = V_AUG - HEAD_DIM
    ones_row = jnp.where(lax.broadcasted_iota(jnp.int32, (pad, tm), 0) == 0, 1.0, 0.0)
    for j in range(N_KV):
        blk = jnp.concatenate([vt[j * HEAD_DIM:(j + 1) * HEAD_DIM, :], ones_row], axis=0)
        vta_ref[0, j * V_AUG:(j + 1) * V_AUG, :] = blk.astype(BF16)


def _pre_fox_prompt_kernel(x_ref, g_ref, wqt_ref, wk_ref, wkt_ref, wvt_ref, wf_ref, wft_ref, wz_ref,
                           bf_ref, bft_ref, low_ref, upp_ref, pk_ref, onek_ref,
                           qta_ref, ka_ref, vta_ref, kt_ref, vt_ref, lft_ref, gate_ref, ccarry, rcarry, *, per):
    @pl.when(pl.program_id(0) % per == 0)
    def _():
        ccarry[...] = jnp.zeros_like(ccarry)
        rcarry[...] = jnp.zeros_like(rcarry)

    tm = x_ref.shape[0]
    h = _rms_norm(x_ref[...], g_ref[...]).astype(BF16)
    lf = _log_sigmoid(_dot(h, wf_ref[...]) + bf_ref[...])
    lft = _log_sigmoid(_dot_nt(wft_ref[...], h) + bft_ref[...])
    lft_ref[0] = lft
    lp = _split3(lf)
    ccol = _dot(low_ref[...], lp[0]) + _dot(low_ref[...], lp[1]) + _dot(low_ref[...], lp[2]) + ccarry[...]
    crow = _dot3(_split3(lft), upp_ref[...]) + rcarry[...]
    ccarry[...] = ccol[tm - 1:tm, :]
    rcarry[...] = crow[:, tm - 1:tm]
    cp = _split3(ccol)
    ka = _dot(h, wk_ref[...]) + onek_ref[...]
    for n in range(N_PIECES):
        ka = ka + _dot(cp[n], pk_ref[n])
    ka_ref[...] = ka.astype(BF16)
    _store_q_aug(qta_ref, _dot_nt(wqt_ref[...], h), [p.astype(F32) for p in _split3(crow)])
    kt_ref[0] = _dot_nt(wkt_ref[...], h)
    vt = _dot_nt(wvt_ref[...], h)
    vt_ref[0] = vt
    _store_v_aug(vta_ref, vt)
    gate_ref[...] = _silu(_dot(h, wz_ref[...])).astype(BF16)


def _bias_placement():
    pk = np.zeros((N_PIECES, N_HEADS, K_AUG_LANES), np.float32)
    onek = np.zeros((1, K_AUG_LANES), np.float32)
    for head in range(N_HEADS):
        kvh, g = divmod(head, GROUP)
        slot = kvh * QK_AUG + HEAD_DIM + BIAS_SLOTS * g
        for n in range(N_PIECES):
            onek[0, slot + n] = 1.0
            pk[n, head, slot + N_PIECES + n] = -1.0
    return jnp.asarray(pk, BF16), jnp.asarray(onek)


def _attn_weights(wq, wk, wv):
    wk_pad = jnp.pad(wk.reshape(D_MODEL, N_KV, HEAD_DIM), ((0, 0), (0, 0), (0, QK_AUG - HEAD_DIM)))
    return dict(wqt=(wq * ATT_SCALE).T.astype(BF16), wk=wk_pad.reshape(D_MODEL, K_AUG_LANES).astype(BF16),
                wkt=wk.T.astype(BF16), wvt=wv.T.astype(BF16))


def _pre_fox_prompt(x, g, w, nb):
    rows = x.shape[0]
    seq = rows // nb
    tm = min(256, seq)
    per = seq // tm
    row, fmaj = _row_specs(rows, nb, tm)
    consts = [_tri(tm, False, False), _tri(tm, False, True), *_bias_placement()]
    names = ("wqt", "wk", "wkt", "wvt", "wf", "wft", "wz", "bf", "bft")
    return pl.pallas_call(
        functools.partial(_pre_fox_prompt_kernel, per=per),
        grid=(rows // tm,),
        in_specs=[row(D_MODEL), _full((1, D_MODEL))] + [_full(w[n].shape) for n in names]
                 + [_full(c.shape) for c in consts],
        out_specs=[fmaj(Q_AUG_ROWS), row(K_AUG_LANES), fmaj(N_KV * V_AUG), fmaj(KV_WIDTH), fmaj(KV_WIDTH),
                   fmaj(N_HEADS), row(D_MODEL)],
        out_shape=[jax.ShapeDtypeStruct((nb, Q_AUG_ROWS, seq), BF16), jax.ShapeDtypeStruct((rows, K_AUG_LANES), BF16),
                   jax.ShapeDtypeStruct((nb, N_KV * V_AUG, seq), BF16),
                   jax.ShapeDtypeStruct((nb, KV_WIDTH, seq), F32), jax.ShapeDtypeStruct((nb, KV_WIDTH, seq), F32),
                   jax.ShapeDtypeStruct((nb, N_HEADS, seq), F32), jax.ShapeDtypeStruct((rows, D_MODEL), BF16)],
        scratch_shapes=[pltpu.VMEM((1, N_HEADS), F32), pltpu.VMEM((N_HEADS, 1), F32)],
        compiler_params=_cparams("arbitrary"),
        name="pre_fox_prompt",
    )(x, g, *[w[n] for n in names], *consts)


def _pre_swa_prompt_kernel(x_ref, g_ref, wqt_ref, wk_ref, wkt_ref, wvt_ref, wz_ref,
                           cos_ref, sa_ref, sb_ref, cost_ref, sint_ref,
                           qta_ref, ka_ref, vta_ref, kt_ref, vt_ref, gate_ref):
    h = _rms_norm(x_ref[...], g_ref[...]).astype(BF16)
    cos_t, sin_t = cost_ref[...], sint_ref[...]
    _store_q_aug(qta_ref, _rope_fmaj(_dot_nt(wqt_ref[...], h), cos_t, sin_t, HEAD_DIM), None)
    ka_ref[...] = _rope_rows(_dot(h, wk_ref[...]), cos_ref[...], sa_ref[...], sb_ref[...]).astype(BF16)
    kt_ref[0] = _rope_fmaj(_dot_nt(wkt_ref[...], h), cos_t, sin_t, HEAD_DIM)
    vt = _dot_nt(wvt_ref[...], h)
    vt_ref[0] = vt
    _store_v_aug(vta_ref, vt)
    gate_ref[...] = _silu(_dot(h, wz_ref[...])).astype(BF16)


def _pre_swa_prompt(x, g, w, tables, nb):
    rows = x.shape[0]
    seq = rows // nb
    tm = min(256, seq)
    per = seq // tm
    row, fmaj = _row_specs(rows, nb, tm)
    tab = pl.BlockSpec((tm, LANES), lambda i: (i % per, 0))
    tab_t = pl.BlockSpec((ROT_HALF, tm), lambda i: (0, i % per))
    names = ("wqt", "wk", "wkt", "wvt", "wz")
    return pl.pallas_call(
        _pre_swa_prompt_kernel,
        grid=(rows // tm,),
        in_specs=[row(D_MODEL), _full((1, D_MODEL))] + [_full(w[n].shape) for n in names]
                 + [tab, tab, tab, tab_t, tab_t],
        out_specs=[fmaj(Q_AUG_ROWS), row(K_AUG_LANES), fmaj(N_KV * V_AUG), fmaj(KV_WIDTH), fmaj(KV_WIDTH),
                   row(D_MODEL)],
        out_shape=[jax.ShapeDtypeStruct((nb, Q_AUG_ROWS, seq), BF16), jax.ShapeDtypeStruct((rows, K_AUG_LANES), BF16),
                   jax.ShapeDtypeStruct((nb, N_KV * V_AUG, seq), BF16),
                   jax.ShapeDtypeStruct((nb, KV_WIDTH, seq), F32), jax.ShapeDtypeStruct((nb, KV_WIDTH, seq), F32),
                   jax.ShapeDtypeStruct((rows, D_MODEL), BF16)],
        compiler_params=_cparams("parallel"),
        name="pre_swa_prompt",
    )(x, g, *[w[n] for n in names], *tables)


def _pre_fox_decode_kernel(x_ref, g_ref, wq_ref, wkt_ref, wvt_ref, wft_ref, wz_ref, bft_ref,
                           q_ref, kt_ref, vt_ref, lft_ref, gate_ref):
    h = _rms_norm(x_ref[...], g_ref[...]).astype(BF16)
    q_ref[...] = _dot(h, wq_ref[...]).astype(BF16)
    kt_ref[0] = _dot_nt(wkt_ref[...], h)
    vt_ref[0] = _dot_nt(wvt_ref[...], h)
    lft_ref[0] = _log_sigmoid(_dot_nt(wft_ref[...], h) + bft_ref[...])
    gate_ref[...] = _silu(_dot(h, wz_ref[...])).astype(BF16)


def _pre_fox_decode(x, g, w):
    rows = x.shape[0]
    row, fmaj = _row_specs(rows, 1, rows)
    names = ("wq", "wkt", "wvt", "wft", "wz", "bft")
    return pl.pallas_call(
        _pre_fox_decode_kernel,
        grid=(1,),
        in_specs=[row(D_MODEL), _full((1, D_MODEL))] + [_full(w[n].shape) for n in names],
        out_specs=[row(D_MODEL), fmaj(KV_WIDTH), fmaj(KV_WIDTH), fmaj(N_HEADS), row(D_MODEL)],
        out_shape=[jax.ShapeDtypeStruct((rows, D_MODEL), BF16),
                   jax.ShapeDtypeStruct((1, KV_WIDTH, rows), F32), jax.ShapeDtypeStruct((1, KV_WIDTH, rows), F32),
                   jax.ShapeDtypeStruct((1, N_HEADS, rows), F32), jax.ShapeDtypeStruct((rows, D_MODEL), BF16)],
        compiler_params=_cparams("arbitrary"),
        name="pre_fox_decode",
    )(x, g, *[w[n] for n in names])


def _pre_swa_decode_kernel(x_ref, g_ref, wq_ref, wkt_ref, wvt_ref, wz_ref, cos_ref, sa_ref, sb_ref,
                           cost_ref, sint_ref, q_ref, kt_ref, vt_ref, gate_ref):
    h = _rms_norm(x_ref[...], g_ref[...]).astype(BF16)
    q_ref[...] = _rope_rows(_dot(h, wq_ref[...]), cos_ref[...], sa_ref[...], sb_ref[...]).astype(BF16)
    kt_ref[0] = _rope_fmaj(_dot_nt(wkt_ref[...], h), cost_ref[...], sint_ref[...], HEAD_DIM)
    vt_ref[0] = _dot_nt(wvt_ref[...], h)
    gate_ref[...] = _silu(_dot(h, wz_ref[...])).astype(BF16)


def _pre_swa_decode(x, g, w, tables):
    rows = x.shape[0]
    row, fmaj = _row_specs(rows, 1, rows)
    names = ("wq", "wkt", "wvt", "wz")
    return pl.pallas_call(
        _pre_swa_decode_kernel,
        grid=(1,),
        in_specs=[row(D_MODEL), _full((1, D_MODEL))] + [_full(w[n].shape) for n in names]
                 + [_full(t.shape) for t in tables],
        out_specs=[row(D_MODEL), fmaj(KV_WIDTH), fmaj(KV_WIDTH), row(D_MODEL)],
        out_shape=[jax.ShapeDtypeStruct((rows, D_MODEL), BF16),
                   jax.ShapeDtypeStruct((1, KV_WIDTH, rows), F32), jax.ShapeDtypeStruct((1, KV_WIDTH, rows), F32),
                   jax.ShapeDtypeStruct((rows, D_MODEL), BF16)],
        compiler_params=_cparams("arbitrary"),
        name="pre_swa_decode",
    )(x, g, *[w[n] for n in names], *tables)


def _pre_conv_kernel(x_ref, g_ref, wa_ref, wg_ref, wz_ref, glu_ref, gate_ref):
    h = _rms_norm(x_ref[...], g_ref[...]).astype(BF16)
    glu_ref[...] = _dot(h, wa_ref[...]) * _sigmoid(_dot(h, wg_ref[...]))
    gate_ref[...] = _silu(_dot(h, wz_ref[...])).astype(BF16)


def _pre_conv(x, g, w):
    rows = x.shape[0]
    tm = min(512, rows)
    row = pl.BlockSpec((tm, D_MODEL), lambda i: (i, 0))
    return pl.pallas_call(
        _pre_conv_kernel,
        grid=(rows // tm,),
        in_specs=[row, _full((1, D_MODEL)), _full(w["wa"].shape), _full(w["wg"].shape), _full(w["wz"].shape)],
        out_specs=[row, row],
        out_shape=[jax.ShapeDtypeStruct((rows, D_MODEL), F32), jax.ShapeDtypeStruct((rows, D_MODEL), BF16)],
        compiler_params=_cparams("parallel"),
        name="pre_conv",
    )(x, g, w["wa"], w["wg"], w["wz"])


def _post_mix_kernel(*refs, gated, final):
    refs = list(refs)
    a_ref = refs.pop(0)
    gate_ref = refs.pop(0) if gated else None
    x_ref, p_ref, wo_ref, ng_ref, wg_ref, bg_ref, wp_ref = refs[:7]
    fg_ref = refs[7] if final else None
    o_ref = refs[-1]
    a = a_ref[...]
    if gated:
        a = (a * gate_ref[...].astype(F32)).astype(BF16)
    x1 = x_ref[...] + _dot(a, wo_ref[...])
    h = _rms_norm(x1, ng_ref[...]).astype(BF16)
    gate2 = _sigmoid(_dot(h, wg_ref[...]) + bg_ref[...])
    x2 = x1 + gate2 * _dot(p_ref[...].astype(BF16), wp_ref[...])
    if final:
        x2 = _rms_norm(x2, fg_ref[...])
    o_ref[...] = x2


def _post_mix(a, gate, x, p_all, layer, w, final_g):
    rows = x.shape[0]
    tm = min(512, rows)
    row = lambda width: pl.BlockSpec((tm, width), lambda i: (i, 0))
    gated, final = gate is not None, final_g is not None
    pe = p_all.shape[2]
    args = [a] + ([gate] if gated else []) + [x, p_all, w["wo"], w["ng"], w["wg"], w["bg"], w["wp"]]
    specs = [row(D_MODEL)] * (2 if gated else 1)
    specs += [row(D_MODEL), pl.BlockSpec((None, tm, pe), lambda i: (layer, i, 0)),
              _full((D_MODEL, D_MODEL)), _full((1, D_MODEL)), _full((D_MODEL, D_MODEL)),
              _full((1, D_MODEL)), _full((pe, D_MODEL))]
    if final:
        args.append(final_g)
        specs.append(_full((1, D_MODEL)))
    return pl.pallas_call(
        functools.partial(_post_mix_kernel, gated=gated, final=final),
        grid=(rows // tm,),
        in_specs=specs,
        out_specs=row(D_MODEL),
        out_shape=jax.ShapeDtypeStruct((rows, D_MODEL), F32),
        compiler_params=_cparams("parallel"),
        name="post_mix",
    )(*args)


def _softmax_step(st, m, acc, vta):
    m_new = jnp.maximum(m, jnp.max(st, axis=0, keepdims=True))
    alpha = jnp.exp(m - m_new)
    p = jnp.exp(st - m_new).astype(BF16)
    return m_new, alpha * acc + _dot(vta, p)


def _finish_heads(accs, denoms, gate_ref, o_ref):
    outs = [(acc[0:HEAD_DIM, :] / den).T for acc, den in zip(accs, denoms)]
    o_ref[...] = (jnp.concatenate(outs, axis=1) * gate_ref[...].astype(F32)).astype(BF16)


def _fox_prompt_kernel(qta_ref, ka_ref, vta_ref, gate_ref, o_ref, *, tq):
    i = pl.program_id(2)
    half = tq // 2
    qts = [qta_ref[0, g * QK_AUG:(g + 1) * QK_AUG, :] for g in range(GROUP)]

    def step(j, carry):
        off = pl.multiple_of(j * tq, tq)
        ka = ka_ref[pl.ds(off, tq), :]
        vta = vta_ref[0, :, pl.ds(off, tq)]
        sts = [_dot(ka, qts[g]) for g in range(GROUP)]
        return tuple(_softmax_step(sts[g], carry[g][0], carry[g][1], vta) for g in range(GROUP))

    init = tuple((jnp.full((1, tq), -jnp.inf, F32), jnp.zeros((V_AUG, tq), F32)) for _ in range(GROUP))
    carry = lax.fori_loop(0, i, step, init)

    off = pl.multiple_of(i * tq, tq)
    ka = ka_ref[pl.ds(off, tq), :]
    vta = vta_ref[0, :, pl.ds(off, tq)]
    tri_a = (lax.broadcasted_iota(jnp.int32, (half, half), 0) <= lax.broadcasted_iota(jnp.int32, (half, half), 1))
    tri_b = (lax.broadcasted_iota(jnp.int32, (tq, half), 0)
             <= lax.broadcasted_iota(jnp.int32, (tq, half), 1) + half)
    sts_a = [jnp.where(tri_a, _dot(ka[0:half, :], qts[g][:, 0:half]), -jnp.inf) for g in range(GROUP)]
    sts_b = [jnp.where(tri_b, _dot(ka, qts[g][:, half:]), -jnp.inf) for g in range(GROUP)]
    accs = []
    for g in range(GROUP):
        m, acc = carry[g]
        _, acc_a = _softmax_step(sts_a[g], m[:, 0:half], acc[:, 0:half], vta[:, 0:half])
        _, acc_b = _softmax_step(sts_b[g], m[:, half:], acc[:, half:], vta)
        accs.append(jnp.concatenate([acc_a, acc_b], axis=1))
    _finish_heads(accs, [acc[HEAD_DIM:HEAD_DIM + 1, :] for acc in accs], gate_ref, o_ref)


def _fox_prompt(qta, ka, vta, gate):
    batch, _, seq = qta.shape
    rows = batch * seq
    tq = min(512, seq)
    assert seq % tq == 0 and tq % (2 * LANES) == 0
    nq = seq // tq
    width = GROUP * HEAD_DIM
    qspec = pl.BlockSpec((tq, width), lambda b, h, i: (b * nq + i, h))
    return pl.pallas_call(
        functools.partial(_fox_prompt_kernel, tq=tq),
        grid=(batch, N_KV, nq),
        in_specs=[pl.BlockSpec((1, GROUP * QK_AUG, tq), lambda b, h, i: (b, h, i)),
                  pl.BlockSpec((seq, QK_AUG), lambda b, h, i: (b, h)),
                  pl.BlockSpec((1, V_AUG, seq), lambda b, h, i: (b, h, 0)),
                  qspec],
        out_specs=qspec,
        out_shape=jax.ShapeDtypeStruct((rows, D_MODEL), BF16),
        compiler_params=_cparams("parallel", "parallel", "arbitrary"),
        name="fox_prompt",
    )(qta, ka, vta, gate)


SWA_QB = 4


def _swa_prompt_kernel(qta_ref, ka_ref, vta_ref, sink_ref, gate_ref, o_ref):
    n0 = pl.program_id(2) * SWA_QB
    ki = lax.broadcasted_iota(jnp.int32, (2 * Q_BLOCK, Q_BLOCK), 0)
    qi = lax.broadcasted_iota(jnp.int32, (2 * Q_BLOCK, Q_BLOCK), 1)
    starts, sts = [], []
    for r in range(SWA_QB):
        n = n0 + r
        start = pl.multiple_of(jnp.maximum(n - 1, 0) * Q_BLOCK, Q_BLOCK)
        rel = (n * Q_BLOCK + qi) - (start + ki)
        allowed = (rel >= 0) & (rel < WINDOW)
        ka = ka_ref[pl.ds(start, 2 * Q_BLOCK), :]
        starts.append(start)
        sts.append([jnp.where(allowed, _dot(ka, qta_ref[0, g * QK_AUG:(g + 1) * QK_AUG,
                                                        r * Q_BLOCK:(r + 1) * Q_BLOCK]), -jnp.inf)
                    for g in range(GROUP)])
    sinks = [sink_ref[0, :, g:g + 1] for g in range(GROUP)]
    ms = [[jnp.maximum(jnp.max(sts[r][g], axis=0, keepdims=True), sinks[g]) for g in range(GROUP)]
          for r in range(SWA_QB)]
    ps = [[jnp.exp(sts[r][g] - ms[r][g]).astype(BF16) for g in range(GROUP)] for r in range(SWA_QB)]
    for r in range(SWA_QB):
        vta = vta_ref[0, :, pl.ds(starts[r], 2 * Q_BLOCK)]
        accs = [_dot(vta, ps[r][g]) for g in range(GROUP)]
        denoms = [accs[g][HEAD_DIM:HEAD_DIM + 1, :] + jnp.exp(sinks[g] - ms[r][g]) for g in range(GROUP)]
        rows = slice(r * Q_BLOCK, (r + 1) * Q_BLOCK)
        _finish_heads(accs, denoms, gate_ref.at[rows, :], o_ref.at[rows, :])


def _swa_prompt(qta, ka, vta, sinks, gate):
    batch, _, seq = qta.shape
    rows = batch * seq
    tq = SWA_QB * Q_BLOCK
    nq = seq // tq
    width = GROUP * HEAD_DIM
    qspec = pl.BlockSpec((tq, width), lambda b, h, i: (b * nq + i, h))
    return pl.pallas_call(
        _swa_prompt_kernel,
        grid=(batch, N_KV, nq),
        in_specs=[pl.BlockSpec((1, GROUP * QK_AUG, tq), lambda b, h, i: (b, h, i)),
                  pl.BlockSpec((seq, QK_AUG), lambda b, h, i: (b, h)),
                  pl.BlockSpec((1, V_AUG, seq), lambda b, h, i: (b, h, 0)),
                  pl.BlockSpec((1, 1, GROUP), lambda b, h, i: (h, 0, 0)),
                  qspec],
        out_specs=qspec,
        out_shape=jax.ShapeDtypeStruct((rows, D_MODEL), BF16),
        compiler_params=_cparams("parallel", "parallel", "arbitrary"),
        name="swa_prompt",
    )(qta, ka, vta, sinks.reshape(N_KV, 1, GROUP), gate)


def _block_diag_mask():
    r = lax.broadcasted_iota(jnp.int32, (DEC_ROWS, KV_WIDTH), 0)
    c = lax.broadcasted_iota(jnp.int32, (DEC_ROWS, KV_WIDTH), 1)
    return ((r % N_HEADS) // GROUP) == (c // HEAD_DIM)


def _fold_heads(o_full):
    om = jnp.where(_block_diag_mask(), o_full, 0.0)
    return (om[:, 0:HEAD_DIM] + om[:, HEAD_DIM:2 * HEAD_DIM]
            + om[:, 2 * HEAD_DIM:3 * HEAD_DIM] + om[:, 3 * HEAD_DIM:4 * HEAD_DIM])


def _new_token_mask():
    lane = lax.broadcasted_iota(jnp.int32, (DEC_ROWS, PAGE), 1)
    t = lax.broadcasted_iota(jnp.int32, (DEC_ROWS, PAGE), 0) // N_HEADS
    return (lane >= NEW_LANE) & (lane - NEW_LANE <= t)


def _fox_decode_kernel(pt_ref, qbd_ref, knew_ref, vnew_ref, lfnew_ref, kc_ref, vc_ref, lc_ref, o_ref,
                       kbuf, vbuf, lbuf, sems, *, layer, cp):
    b = pl.program_id(0)
    nb = pl.num_programs(0)
    nch = pt_ref.shape[1] // cp

    def copies(seq, ci, slot):
        base = (nch - 1 - ci) * cp
        out = []
        for p in range(cp):
            page = pt_ref[seq, base + p]
            out.append(pltpu.make_async_copy(kc_ref.at[layer, page], kbuf.at[slot, p], sems.at[0, slot]))
            out.append(pltpu.make_async_copy(vc_ref.at[layer, page], vbuf.at[slot, p], sems.at[1, slot]))
            out.append(pltpu.make_async_copy(lc_ref.at[layer, page], lbuf.at[slot, p], sems.at[2, slot]))
        return out

    total = nb * nch

    def start_chunk(g):
        for c in copies(g // nch, g % nch, g % DEC_SLOTS):
            c.start()

    @pl.when(b == 0)
    def _():
        for g in range(DEC_SLOTS - 1):
            start_chunk(g)

    qbd = qbd_ref[0]

    x = lfnew_ref[0]
    cn = x + pltpu.roll(x, 1, 1) + pltpu.roll(x, 2, 1) + pltpu.roll(x, 3, 1)
    s_new = _dot(qbd, knew_ref[0].astype(BF16))
    cqs = [cn[:, NEW_LANE + t:NEW_LANE + t + 1] for t in range(DEC_T)]
    s_new = s_new + jnp.concatenate([cqs[t] - cn for t in range(DEC_T)], axis=0)
    s_new = jnp.where(_new_token_mask(), s_new, -jnp.inf)
    cq = jnp.concatenate(cqs, axis=0)
    m = jnp.max(s_new, axis=-1, keepdims=True)
    p = jnp.exp(s_new - m)
    l = jnp.sum(p, axis=-1, keepdims=True)
    acc = _dot_nt(p.astype(BF16), vnew_ref[0].astype(BF16))

    sfx = jnp.concatenate([_tri(PAGE, strict=True, upper=False), jnp.ones((PAGE, LANES), BF16)], axis=1)

    def chunk(ci, carry):
        m, l, acc, later = carry
        g = b * nch + ci
        slot = g % DEC_SLOTS
        for c in copies(b, ci, slot):
            c.wait()

        @pl.when(g + DEC_SLOTS - 1 < total)
        def _():
            start_chunk(g + DEC_SLOTS - 1)

        st = _dot3(_split3(lbuf[slot].reshape(cp * N_HEADS, PAGE)), sfx).reshape(cp, N_HEADS, 2 * LANES)
        bias = [None] * cp
        for pg in reversed(range(cp)):
            bias[pg] = jnp.concatenate([st[pg, :, :LANES] + later] * DEC_T, axis=0) + cq
            later = later + st[pg, :, LANES:]
        ss = [_dot(qbd, kbuf[slot, pg].astype(BF16)) + bias[pg] for pg in range(cp)]
        per = cp // DEC_GROUPS if cp % DEC_GROUPS == 0 else cp
        parts = []
        for lo in range(0, cp, per):
            grp = ss[lo:lo + per]
            m_loc = jnp.max(functools.reduce(jnp.maximum, grp), axis=-1, keepdims=True)
            ps = [jnp.exp(s - m_loc) for s in grp]
            l_loc = jnp.sum(functools.reduce(jnp.add, ps), axis=-1, keepdims=True)
            parts.append((m_loc, l_loc, lo, ps))
        accs = []
        for m_loc, l_loc, lo, ps in parts:
            a = _dot_nt(ps[0].astype(BF16), vbuf[slot, lo].astype(BF16))
            for n in range(1, len(ps)):
                a = a + _dot_nt(ps[n].astype(BF16), vbuf[slot, lo + n].astype(BF16))
            accs.append(a)
        m_new = functools.reduce(jnp.maximum, [m] + [pt[0] for pt in parts])
        alpha = jnp.exp(m - m_new)
        l, acc = alpha * l, alpha * acc
        for (m_loc, l_loc, _, _), a in zip(parts, accs):
            w = jnp.exp(m_loc - m_new)
            l, acc = l + w * l_loc, acc + w * a
        return m_new, l, acc, later

    m, l, acc, _ = lax.fori_loop(0, nch, chunk, (m, l, acc, jnp.zeros((N_HEADS, LANES), F32)))
    o_ref[0] = _fold_heads(acc / l)


def _fox_decode(page_table, qbd, knew, vnew, lfnew, cache_kt, cache_vt, cache_lft, layer):
    batch, n_pages = page_table.shape
    cp = 16
    while n_pages % cp:
        cp //= 2
    assert batch * (n_pages // cp) >= DEC_SLOTS - 1
    seq3 = lambda r, c: pl.BlockSpec((1, r, c), lambda b, pt: (b, 0, 0))
    grid_spec = pltpu.PrefetchScalarGridSpec(
        num_scalar_prefetch=1,
        grid=(batch,),
        in_specs=[seq3(DEC_ROWS, KV_WIDTH), seq3(KV_WIDTH, PAGE), seq3(KV_WIDTH, PAGE), seq3(N_HEADS, PAGE),
                  pl.BlockSpec(memory_space=pl.ANY), pl.BlockSpec(memory_space=pl.ANY),
                  pl.BlockSpec(memory_space=pl.ANY)],
        out_specs=seq3(DEC_ROWS, HEAD_DIM),
        scratch_shapes=[pltpu.VMEM((DEC_SLOTS, cp, KV_WIDTH, PAGE), F32),
                        pltpu.VMEM((DEC_SLOTS, cp, KV_WIDTH, PAGE), F32),
                        pltpu.VMEM((DEC_SLOTS, cp, N_HEADS, PAGE), F32), pltpu.SemaphoreType.DMA((3, DEC_SLOTS))],
    )
    return pl.pallas_call(
        functools.partial(_fox_decode_kernel, layer=layer, cp=cp),
        grid_spec=grid_spec,
        out_shape=jax.ShapeDtypeStruct((batch, DEC_ROWS, HEAD_DIM), F32),
        compiler_params=_cparams("arbitrary"),
        name="fox_decode",
    )(page_table, qbd, knew, vnew, lfnew, cache_kt, cache_vt, cache_lft)


def _swa_decode_kernel(qbd_ref, knew_ref, vnew_ref, sk_ref, sv_ref, sink_ref, o_ref, nk_ref, nv_ref):
    lane = lax.broadcasted_iota(jnp.int32, (DEC_ROWS, PAGE), 1)
    t = lax.broadcasted_iota(jnp.int32, (DEC_ROWS, PAGE), 0) // N_HEADS
    keep_new = lax.broadcasted_iota(jnp.int32, (KV_WIDTH, PAGE), 1) >= NEW_LANE
    sink = sink_ref[...]
    for s in range(qbd_ref.shape[0]):
        qbd = qbd_ref[s]
        sk, sv, kn, vn = sk_ref[0, s], sv_ref[0, s], knew_ref[s], vnew_ref[s]
        s_old = jnp.where(lane > t, _dot(qbd, sk.astype(BF16)), -jnp.inf)
        s_new = jnp.where(_new_token_mask(), _dot(qbd, kn.astype(BF16)), -jnp.inf)
        m = jnp.maximum(jnp.maximum(jnp.max(s_old, axis=-1, keepdims=True),
                                    jnp.max(s_new, axis=-1, keepdims=True)), sink)
        p_old = jnp.exp(s_old - m)
        p_new = jnp.exp(s_new - m)
        denom = (jnp.sum(p_old, axis=-1, keepdims=True) + jnp.sum(p_new, axis=-1, keepdims=True)
                 + jnp.exp(sink - m))
        o_full = _dot_nt(p_old.astype(BF16), sv.astype(BF16)) + _dot_nt(p_new.astype(BF16), vn.astype(BF16))
        o_ref[s] = _fold_heads(o_full / denom)
        nk_ref[s] = jnp.where(keep_new, kn, pltpu.roll(sk, NEW_LANE, 1))
        nv_ref[s] = jnp.where(keep_new, vn, pltpu.roll(sv, NEW_LANE, 1))


def _swa_decode(qbd, knew, vnew, state_kt, state_vt, sink_rows, layer):
    batch, buf = state_kt.shape[1], state_kt.shape[3]
    assert buf == WINDOW == PAGE
    sb = 4
    while batch % sb:
        sb //= 2
    seq3 = lambda r, c: pl.BlockSpec((sb, r, c), lambda b: (b, 0, 0))
    st = pl.BlockSpec((1, sb, KV_WIDTH, buf), lambda b: (layer, b, 0, 0))
    return pl.pallas_call(
        _swa_decode_kernel,
        grid=(batch // sb,),
        in_specs=[seq3(DEC_ROWS, KV_WIDTH), seq3(KV_WIDTH, PAGE), seq3(KV_WIDTH, PAGE), st, st,
                  _full((DEC_ROWS, 1))],
        out_specs=[seq3(DEC_ROWS, HEAD_DIM), seq3(KV_WIDTH, buf), seq3(KV_WIDTH, buf)],
        out_shape=[jax.ShapeDtypeStruct((batch, DEC_ROWS, HEAD_DIM), F32),
                   jax.ShapeDtypeStruct((batch, KV_WIDTH, buf), F32),
                   jax.ShapeDtypeStruct((batch, KV_WIDTH, buf), F32)],
        compiler_params=_cparams("parallel"),
        name="swa_decode",
    )(qbd, knew, vnew, state_kt, state_vt, sink_rows)


HALO = 32
CONV_CHUNK = 32


def _conv_tail(y, gate, lng, lnb):
    yc = y - jnp.mean(y, axis=-1, keepdims=True)
    yn = yc * lax.rsqrt(jnp.mean(yc * yc, axis=-1, keepdims=True) + NORM_EPS) * lng + lnb
    return (_silu(yn) * gate.astype(F32)).astype(BF16)


def _conv_prompt_kernel(x_ref, g_ref, wa_ref, wg_ref, wz_ref, dw_ref, db_ref, lng_ref, lnb_ref,
                        a_ref, st_ref, buf, shifted, gbuf):
    i = pl.program_id(1)
    tm = x_ref.shape[0]
    h = _rms_norm(x_ref[...], g_ref[...]).astype(BF16)
    glu = _dot(h, wa_ref[...]) * _sigmoid(_dot(h, wg_ref[...]))
    gate = _silu(_dot(h, wz_ref[...]))

    @pl.when(i == 0)
    def _():
        buf[0:HALO, :] = jnp.zeros((HALO, D_MODEL), F32)

    buf[HALO:HALO + tm, :] = glu
    gbuf[...] = gate
    first = HALO - CONV_HIST
    for r in range(SUBLANES):
        span = tm + SUBLANES * ((CONV_HIST - r) // SUBLANES)
        shifted[r, 0:span, :] = buf[first + r:first + r + span, :]

    def chunk(c, _):
        off = pl.multiple_of(c * CONV_CHUNK, CONV_CHUNK)
        y = jnp.zeros((CONV_CHUNK, D_MODEL), F32) + db_ref[...]
        for w in range(CONV_WIDTH):
            start = off + SUBLANES * (w // SUBLANES)
            y = y + shifted[w % SUBLANES, pl.ds(start, CONV_CHUNK), :] * dw_ref[w:w + 1, :]
        rows = pl.ds(off, CONV_CHUNK)
        a_ref[rows, :] = _conv_tail(y, gbuf[rows, :], lng_ref[...], lnb_ref[...])
        return 0

    lax.fori_loop(0, tm // CONV_CHUNK, chunk, 0, unroll=4)

    @pl.when(i == pl.num_programs(1) - 1)
    def _():
        st_ref[0] = buf[HALO + tm - CONV_HIST:HALO + tm, :]

    buf[0:HALO, :] = buf[tm:tm + HALO, :]


def _conv_prompt(x, g, w, batch, seq):
    rows = batch * seq
    tm = min(256, seq)
    nt = seq // tm
    row = pl.BlockSpec((tm, D_MODEL), lambda b, i: (b * nt + i, 0))
    return pl.pallas_call(
        _conv_prompt_kernel,
        grid=(batch, nt),
        in_specs=[row, _full((1, D_MODEL)), _full(w["wa"].shape), _full(w["wg"].shape), _full(w["wz"].shape),
                  _full((CONV_WIDTH, D_MODEL)), _full((1, D_MODEL)), _full((1, D_MODEL)), _full((1, D_MODEL))],
        out_specs=[row, pl.BlockSpec((1, CONV_HIST, D_MODEL), lambda b, i: (b, 0, 0))],
        out_shape=[jax.ShapeDtypeStruct((rows, D_MODEL), BF16),
                   jax.ShapeDtypeStruct((batch, CONV_HIST, D_MODEL), F32)],
        scratch_shapes=[pltpu.VMEM((HALO + tm, D_MODEL), F32),
                        pltpu.VMEM((SUBLANES, tm + SUBLANES * (CONV_HIST // SUBLANES), D_MODEL), F32),
                        pltpu.VMEM((tm, D_MODEL), F32)],
        compiler_params=_cparams("arbitrary", "arbitrary"),
        name="conv_prompt",
    )(x, g, w["wa"], w["wg"], w["wz"], w["dw"], w["db"], w["lng"], w["lnb"])


def _conv_decode_kernel(glu_ref, gate_ref, st_ref, dw_ref, db_ref, lng_ref, lnb_ref, a_ref, ns_ref, pad):
    sb = glu_ref.shape[1]
    pad[0:CONV_HIST] = st_ref[0]
    pad[CONV_HIST:CONV_HIST + DEC_T] = glu_ref[...]
    y = jnp.zeros((DEC_T, sb, D_MODEL), F32) + db_ref[...]
    for w in range(CONV_WIDTH):
        y = y + pad[w:w + DEC_T] * dw_ref[w:w + 1, :]
    a_ref[...] = _conv_tail(y, gate_ref[...], lng_ref[...], lnb_ref[...])
    ns_ref[...] = pad[DEC_T:DEC_T + CONV_HIST]


def _conv_decode(glu, gate, state, w, layer):
    batch = state.shape[2]
    sb = 16
    while batch % sb:
        sb //= 2
    tok = pl.BlockSpec((DEC_T, sb, D_MODEL), lambda i: (0, i, 0))
    return pl.pallas_call(
        _conv_decode_kernel,
        grid=(batch // sb,),
        in_specs=[tok, tok, pl.BlockSpec((1, CONV_HIST, sb, D_MODEL), lambda i: (layer, 0, i, 0)),
                  _full((CONV_WIDTH, D_MODEL)), _full((1, D_MODEL)), _full((1, D_MODEL)), _full((1, D_MODEL))],
        out_specs=[tok, pl.BlockSpec((CONV_HIST, sb, D_MODEL), lambda i: (0, i, 0))],
        out_shape=[jax.ShapeDtypeStruct((DEC_T, batch, D_MODEL), BF16),
                   jax.ShapeDtypeStruct((CONV_HIST, batch, D_MODEL), F32)],
        scratch_shapes=[pltpu.VMEM((CONV_HIST + DEC_T, sb, D_MODEL), F32)],
        compiler_params=_cparams("parallel"),
        name="conv_decode",
    )(glu.reshape(DEC_T, batch, D_MODEL), gate.reshape(DEC_T, batch, D_MODEL), state,
      w["dw"], w["db"], w["lng"], w["lnb"])


def _rope_tables(pos):
    inv_freq = ROPE_THETA ** (-jnp.arange(ROT_HALF, dtype=F32) / ROT_HALF)
    ang = pos.astype(F32)[:, None] * inv_freq[None, :]
    cos, sin = jnp.cos(ang), jnp.sin(ang)
    n = pos.shape[0]
    rest = jnp.zeros((n, HEAD_DIM - ROT_DIM), F32)
    zero = jnp.zeros((n, ROT_HALF), F32)
    cos_h = jnp.concatenate([cos, cos, rest + 1.0], axis=1)
    sa_h = jnp.concatenate([-sin, zero, rest], axis=1)
    sb_h = jnp.concatenate([zero, sin, rest], axis=1)
    rep = LANES // HEAD_DIM
    return tuple(jnp.tile(t, (1, rep)) for t in (cos_h, sa_h, sb_h)) + (cos.T, sin.T)


def _decode_queries(q_rows, batch):
    q4 = q_rows.reshape(DEC_T, batch, N_HEADS, 1, HEAD_DIM).transpose(1, 0, 2, 3, 4)
    own = (jnp.arange(N_HEADS)[:, None] // GROUP) == jnp.arange(N_KV)[None, :]
    return jnp.where(own[None, None, :, :, None], q4, jnp.zeros((), q_rows.dtype)).reshape(batch, DEC_ROWS, KV_WIDTH)


def _new_token_pages(xt, batch):
    width = xt.shape[1]
    per_seq = xt.reshape(width, DEC_T, batch).transpose(2, 0, 1)
    return jnp.pad(per_seq, ((0, 0), (0, 0), (NEW_LANE, 0)))


def _decode_out_rows(o, batch):
    return o.reshape(batch, DEC_T, D_MODEL).transpose(1, 0, 2).reshape(DEC_T * batch, D_MODEL)


def _sample_major(xt, batch, inner):
    return xt.reshape(inner + (DEC_T, batch)).transpose((len(inner) + 1, len(inner)) + tuple(range(len(inner))))


def kernel(x_prompt, x_sample, cache_fox_k, cache_fox_v, cache_fox_lf, page_table, state_swa_k, state_swa_v,
           state_conv, p_prompt, p_sample, norm_g, fox_w_in, fox_b_f, fox_w_out, swa_w_in, swa_sinks, swa_w_out,
           conv_w_in, conv_dw_w, conv_dw_b, conv_ln_g, conv_ln_b, conv_w_out, pe_w_proj, pe_norm_g, pe_w_gate,
           pe_b_gate, final_norm_g):
    batch, seq, _ = x_prompt.shape
    dec_batch, dec_t, _ = x_sample.shape
    assert dec_t == DEC_T and seq % (SWA_QB * Q_BLOCK) == 0
    depth = norm_g.shape[0]
    past_len = page_table.shape[1] * PAGE
    buf = state_swa_k.shape[2]
    rows_p, rows_s = batch * seq, dec_batch * dec_t
    att = N_HEADS * HEAD_DIM

    xp = x_prompt.reshape(rows_p, D_MODEL)
    xs = x_sample.transpose(1, 0, 2).reshape(rows_s, D_MODEL)
    pp_all = p_prompt.reshape(depth, rows_p, -1)
    ps_all = p_sample.transpose(0, 2, 1, 3).reshape(depth, rows_s, -1)
    fmaj = lambda c: c.transpose(0, 1, 3, 4, 2).reshape(c.shape[:2] + (KV_WIDTH, c.shape[2]))
    cache_kt, cache_vt = fmaj(cache_fox_k), fmaj(cache_fox_v)
    cache_lft = cache_fox_lf.transpose(0, 1, 3, 2)
    st_kt, st_vt = fmaj(state_swa_k), fmaj(state_swa_v)
    st_conv = state_conv.transpose(0, 2, 1, 3)
    heads_major = lambda t: t.reshape(t.shape[0], N_KV, HEAD_DIM, t.shape[2]).transpose(0, 3, 1, 2)

    fox_k_p, fox_v_p, fox_lf_p, fox_k_s, fox_v_s, fox_lf_s = [], [], [], [], [], []
    swa_k_p, swa_v_p, swa_k_s, swa_v_s = [], [], [], []
    conv_p, conv_s = [], []

    for i in range(depth):
        kind, j = i % N_MIXERS, i // N_MIXERS
        g = norm_g[i].reshape(1, D_MODEL)
        if kind == 0:
            w_in = fox_w_in[j]
            wq, wk = w_in[:, :att], w_in[:, att:att + KV_WIDTH]
            wv = w_in[:, att + KV_WIDTH:att + 2 * KV_WIDTH]
            wf = w_in[:, att + 2 * KV_WIDTH:att + 2 * KV_WIDTH + N_HEADS]
            w = dict(_attn_weights(wq, wk, wv), wq=(wq * ATT_SCALE).astype(BF16),
                     wf=wf.astype(BF16), wft=wf.T.astype(BF16),
                     wz=w_in[:, att + 2 * KV_WIDTH + N_HEADS:].astype(BF16),
                     bf=fox_b_f[j].reshape(1, N_HEADS), bft=fox_b_f[j].reshape(N_HEADS, 1))
            w_out = fox_w_out[j]
            qta, ka, vta, kt, vt, lft, gate = _pre_fox_prompt(xp, g, w, batch)
            ap = _fox_prompt(qta, ka, vta, gate)
            gate_p = None
            fox_k_p.append(heads_major(kt))
            fox_v_p.append(heads_major(vt))
            fox_lf_p.append(lft.transpose(0, 2, 1))
            q, kt, vt, lft, gate_s = _pre_fox_decode(xs, g, w)
            o = _fox_decode(page_table, _decode_queries(q, dec_batch),
                            _new_token_pages(kt.astype(BF16), dec_batch), _new_token_pages(vt.astype(BF16), dec_batch),
                            _new_token_pages(lft, dec_batch), cache_kt, cache_vt, cache_lft, j)
            a_s = _decode_out_rows(o, dec_batch)
            fox_k_s.append(_sample_major(kt, dec_batch, (N_KV, HEAD_DIM)))
            fox_v_s.append(_sample_major(vt, dec_batch, (N_KV, HEAD_DIM)))
            fox_lf_s.append(_sample_major(lft, dec_batch, (N_HEADS,)))
        elif kind == 1:
            w_in = swa_w_in[j]
            wq, wk = w_in[:, :att], w_in[:, att:att + KV_WIDTH]
            wv = w_in[:, att + KV_WIDTH:att + 2 * KV_WIDTH]
            w = dict(_attn_weights(wq, wk, wv), wq=(wq * ATT_SCALE).astype(BF16),
                     wz=w_in[:, att + 2 * KV_WIDTH:].astype(BF16))
            w_out = swa_w_out[j]
            tables = _rope_tables(jnp.arange(seq, dtype=jnp.int32))
            qta, ka, vta, kt, vt, gate = _pre_swa_prompt(xp, g, w, tables, batch)
            ap = _swa_prompt(qta, ka, vta, swa_sinks[j], gate)
            gate_p = None
            swa_k_p.append(heads_major(kt[:, :, seq - buf:]))
            swa_v_p.append(heads_major(vt[:, :, seq - buf:]))
            pos_s = past_len + jnp.repeat(jnp.arange(DEC_T, dtype=jnp.int32), dec_batch)
            q, kt, vt, gate_s = _pre_swa_decode(xs, g, w, _rope_tables(pos_s))
            sink_rows = jnp.tile(swa_sinks[j].astype(F32), DEC_T).reshape(DEC_ROWS, 1)
            o, nkt, nvt = _swa_decode(_decode_queries(q, dec_batch), _new_token_pages(kt, dec_batch),
                                      _new_token_pages(vt, dec_batch), st_kt, st_vt, sink_rows, j)
            a_s = _decode_out_rows(o, dec_batch)
            swa_k_s.append(heads_major(nkt))
            swa_v_s.append(heads_major(nvt))
        else:
            w_in = conv_w_in[j]
            cd = w_in.shape[1] // 3
            w = dict(wa=w_in[:, :cd].astype(BF16), wg=w_in[:, cd:2 * cd].astype(BF16),
                     wz=w_in[:, 2 * cd:].astype(BF16), dw=conv_dw_w[j], db=conv_dw_b[j].reshape(1, cd),
                     lng=conv_ln_g[j].reshape(1, cd), lnb=conv_ln_b[j].reshape(1, cd))
            w_out = conv_w_out[j]
            ap, st = _conv_prompt(xp, g, w, batch, seq)
            gate_p = None
            conv_p.append(st)
            glu, gate = _pre_conv(xs, g, w)
            a_s, ns = _conv_decode(glu, gate, st_conv, w, j)
            a_s = a_s.reshape(rows_s, D_MODEL)
            gate_s = None
            conv_s.append(ns.transpose(1, 0, 2))
        wpost = dict(wo=w_out.astype(BF16), ng=pe_norm_g[i].reshape(1, D_MODEL), wg=pe_w_gate[i].astype(BF16),
                     bg=pe_b_gate[i].reshape(1, D_MODEL), wp=pe_w_proj[i].astype(BF16))
        fg = final_norm_g.reshape(1, D_MODEL) if i == depth - 1 else None
        xp = _post_mix(ap, gate_p, xp, pp_all, i, wpost, fg)
        xs = _post_mix(a_s, gate_s, xs, ps_all, i, wpost, fg)

    return (xp.reshape(batch, seq, D_MODEL), xs.reshape(DEC_T, dec_batch, D_MODEL).transpose(1, 0, 2),
            jnp.stack(fox_k_p), jnp.stack(fox_v_p), jnp.stack(fox_lf_p),
            jnp.stack(fox_k_s), jnp.stack(fox_v_s), jnp.stack(fox_lf_s),
            jnp.stack(swa_k_p), jnp.stack(swa_v_p), jnp.stack(swa_k_s), jnp.stack(swa_v_s),
            jnp.stack(conv_p), jnp.stack(conv_s))
```

```python
import functools

import numpy as np

import jax
import jax.numpy as jnp
from jax import lax
from jax.experimental import pallas as pl
from jax.experimental.pallas import tpu as pltpu

F32 = jnp.float32
BF16 = jnp.bfloat16

D_MODEL = 1024
HEAD_DIM = 64
N_HEADS = 16
N_KV = 4
GROUP = 4
KV_WIDTH = N_KV * HEAD_DIM
ATT_SCALE = HEAD_DIM ** -0.5
LOG2E = 1.4426950408889634
N_MIXERS = 3
Q_BLOCK = 128
WINDOW = 128
ROPE_THETA = 500000.0
ROT_DIM = HEAD_DIM // 4
ROT_HALF = ROT_DIM // 2
CONV_WIDTH = 31
CONV_HIST = CONV_WIDTH - 1
NORM_EPS = 1e-6
PAGE = 128
LANES = 128
SUBLANES = 8
VMEM_LIMIT = 52 * 1024 * 1024
DEC_T = 4
DEC_ROWS = DEC_T * N_HEADS
NEW_LANE = PAGE - DEC_T
DEC_SLOTS = 3
DEC_GROUPS = 4

QK_AUG = 128
BIAS_SLOTS = 8
N_PIECES = 3
V_AUG = 80
Q_AUG_ROWS = N_HEADS * QK_AUG
K_AUG_LANES = N_KV * QK_AUG

NT_DIMS = (((1,), (1,)), ((), ()))


def _cparams(*sem, flags=None):
    return pltpu.CompilerParams(dimension_semantics=sem, vmem_limit_bytes=VMEM_LIMIT, flags=flags)


def _dot(a, b):
    return jnp.dot(a, b, preferred_element_type=F32)


def _dot_nt(a, b):
    return lax.dot_general(a, b, NT_DIMS, preferred_element_type=F32)


def _sigmoid(x):
    return 1.0 / (1.0 + jnp.exp(-x))


def _silu(x):
    return x * _sigmoid(x)


def _log_sigmoid(x):
    return jnp.minimum(x, 0.0) - jnp.log1p(jnp.exp(-jnp.abs(x)))


def _rms_norm(x, g):
    return x * lax.rsqrt(jnp.mean(x * x, axis=-1, keepdims=True) + NORM_EPS) * g


def _split3(x):
    p1 = x.astype(BF16)
    r1 = x - p1.astype(F32)
    p2 = r1.astype(BF16)
    r2 = r1 - p2.astype(F32)
    return p1, p2, r2.astype(BF16)


def _dot3(pieces, m):
    return _dot(pieces[0], m) + _dot(pieces[1], m) + _dot(pieces[2], m)


def _full(shape):
    n = len(shape)
    return pl.BlockSpec(shape, lambda *_: (0,) * n)


def _tri(n, strict, upper):
    r = lax.broadcasted_iota(jnp.int32, (n, n), 0)
    c = lax.broadcasted_iota(jnp.int32, (n, n), 1)
    if upper:
        keep = (r < c) if strict else (r <= c)
    else:
        keep = (r > c) if strict else (r >= c)
    return jnp.where(keep, 1.0, 0.0).astype(BF16)


def _row_specs(rows, nb, tm):
    per = rows // nb // tm
    row = lambda width: pl.BlockSpec((tm, width), lambda i: (i, 0))
    fmaj = lambda width: pl.BlockSpec((1, width, tm), lambda i: (i // per, 0, i % per))
    return row, fmaj


def _rope_rows(u, cos, sa, sb):
    outs = []
    for c in range(u.shape[1] // LANES):
        blk = u[:, c * LANES:(c + 1) * LANES]
        outs.append(blk * cos + pltpu.roll(blk, LANES - ROT_HALF, 1) * sa + pltpu.roll(blk, ROT_HALF, 1) * sb)
    return jnp.concatenate(outs, axis=1)


def _rope_fmaj(ut, cos_t, sin_t, stride):
    assert ROT_HALF == SUBLANES
    outs = []
    for base in range(0, ut.shape[0], stride):
        x1 = ut[base:base + ROT_HALF, :]
        x2 = ut[base + ROT_HALF:base + ROT_DIM, :]
        outs += [x1 * cos_t - x2 * sin_t, x2 * cos_t + x1 * sin_t, ut[base + ROT_DIM:base + stride, :]]
    return jnp.concatenate(outs, axis=0)


def _store_q_aug(qta_ref, qt, bias_pieces):
    tm = qt.shape[1]
    sub = lax.broadcasted_iota(jnp.int32, (BIAS_SLOTS, tm), 0)
    ones = jnp.where((sub >= N_PIECES) & (sub < 2 * N_PIECES), 1.0, 0.0)
    for head in range(N_HEADS):
        parts = [qt[head * HEAD_DIM:(head + 1) * HEAD_DIM, :]]
        if bias_pieces is None:
            parts.append(jnp.zeros((QK_AUG - HEAD_DIM, tm), F32))
        else:
            g = head % GROUP
            grp = ones
            for n in reversed(range(N_PIECES)):
                grp = jnp.where(sub == n, bias_pieces[n][head:head + 1, :], grp)
            before, after = BIAS_SLOTS * g, QK_AUG - HEAD_DIM - BIAS_SLOTS * (g + 1)
            parts += ([jnp.zeros((before, tm), F32)] if before else []) + [grp]
            parts += [jnp.zeros((after, tm), F32)] if after else []
        qta_ref[0, head * QK_AUG:(head + 1) * QK_AUG, :] = jnp.concatenate(parts, axis=0).astype(BF16)


def _store_v_aug(vta_ref, vt):
    tm = vt.shape[1]
    pad = V_AUG - HEAD_DIM
    ones_row = jnp.where(lax.broadcasted_iota(jnp.int32, (pad, tm), 0) == 0, 1.0, 0.0)
    for j in range(N_KV):
        blk = jnp.concatenate([vt[j * HEAD_DIM:(j + 1) * HEAD_DIM, :], ones_row], axis=0)
        vta_ref[0, j * V_AUG:(j + 1) * V_AUG, :] = blk.astype(BF16)


def _pre_fox_prompt_kernel(x_ref, g_ref, wqt_ref, wk_ref, wkt_ref, wvt_ref, wf_ref, wft_ref, wz_ref,
                           bf_ref, bft_ref, low_ref, upp_ref, pk_ref, onek_ref,
                           qta_ref, ka_ref, vta_ref, kt_ref, vt_ref, lft_ref, gate_ref, ccarry, rcarry, *, per):
    @pl.when(pl.program_id(0) % per == 0)
    def _():
        ccarry[...] = jnp.zeros_like(ccarry)
        rcarry[...] = jnp.zeros_like(rcarry)

    tm = x_ref.shape[0]
    h = _rms_norm(x_ref[...], g_ref[...]).astype(BF16)
    lf = _log_sigmoid(_dot(h, wf_ref[...]) + bf_ref[...])
    lft = _log_sigmoid(_dot_nt(wft_ref[...], h) + bft_ref[...])
    lft_ref[0] = lft
    lp = _split3(lf)
    ccol = _dot(low_ref[...], lp[0]) + _dot(low_ref[...], lp[1]) + _dot(low_ref[...], lp[2]) + ccarry[...]
    crow = _dot3(_split3(lft), upp_ref[...]) + rcarry[...]
    ccarry[...] = ccol[tm - 1:tm, :]
    rcarry[...] = crow[:, tm - 1:tm]
    cp = _split3(ccol * LOG2E)
    ka = _dot(h, wk_ref[...]) + onek_ref[...]
    for n in range(N_PIECES):
        ka = ka + _dot(cp[n], pk_ref[n])
    ka_ref[...] = ka.astype(BF16)
    _store_q_aug(qta_ref, _dot_nt(wqt_ref[...], h), [p.astype(F32) for p in _split3(crow * LOG2E)])
    kt_ref[0] = _dot_nt(wkt_ref[...], h)
    vt = _dot_nt(wvt_ref[...], h)
    vt_ref[0] = vt
    _store_v_aug(vta_ref, vt)
    gate_ref[...] = _silu(_dot(h, wz_ref[...])).astype(BF16)


def _bias_placement():
    pk = np.zeros((N_PIECES, N_HEADS, K_AUG_LANES), np.float32)
    onek = np.zeros((1, K_AUG_LANES), np.float32)
    for head in range(N_HEADS):
        kvh, g = divmod(head, GROUP)
        slot = kvh * QK_AUG + HEAD_DIM + BIAS_SLOTS * g
        for n in range(N_PIECES):
            onek[0, slot + n] = 1.0
            pk[n, head, slot + N_PIECES + n] = -1.0
    return jnp.asarray(pk, BF16), jnp.asarray(onek)


def _attn_weights(wq, wk, wv):
    wk_pad = jnp.pad(wk.reshape(D_MODEL, N_KV, HEAD_DIM), ((0, 0), (0, 0), (0, QK_AUG - HEAD_DIM)))
    return dict(wqt=(wq * (ATT_SCALE * LOG2E)).T.astype(BF16), wk=wk_pad.reshape(D_MODEL, K_AUG_LANES).astype(BF16),
                wkt=wk.T.astype(BF16), wvt=wv.T.astype(BF16))


def _pre_fox_prompt(x, g, w, nb):
    rows = x.shape[0]
    seq = rows // nb
    tm = min(256, seq)
    per = seq // tm
    row, fmaj = _row_specs(rows, nb, tm)
    consts = [_tri(tm, False, False), _tri(tm, False, True), *_bias_placement()]
    names = ("wqt", "wk", "wkt", "wvt", "wf", "wft", "wz", "bf", "bft")
    return pl.pallas_call(
        functools.partial(_pre_fox_prompt_kernel, per=per),
        grid=(rows // tm,),
        in_specs=[row(D_MODEL), _full((1, D_MODEL))] + [_full(w[n].shape) for n in names]
                 + [_full(c.shape) for c in consts],
        out_specs=[fmaj(Q_AUG_ROWS), row(K_AUG_LANES), fmaj(N_KV * V_AUG), fmaj(KV_WIDTH), fmaj(KV_WIDTH),
                   fmaj(N_HEADS), row(D_MODEL)],
        out_shape=[jax.ShapeDtypeStruct((nb, Q_AUG_ROWS, seq), BF16), jax.ShapeDtypeStruct((rows, K_AUG_LANES), BF16),
                   jax.ShapeDtypeStruct((nb, N_KV * V_AUG, seq), BF16),
                   jax.ShapeDtypeStruct((nb, KV_WIDTH, seq), F32), jax.ShapeDtypeStruct((nb, KV_WIDTH, seq), F32),
                   jax.ShapeDtypeStruct((nb, N_HEADS, seq), F32), jax.ShapeDtypeStruct((rows, D_MODEL), BF16)],
        scratch_shapes=[pltpu.VMEM((1, N_HEADS), F32), pltpu.VMEM((N_HEADS, 1), F32)],
        compiler_params=_cparams("arbitrary"),
        name="pre_fox_prompt",
    )(x, g, *[w[n] for n in names], *consts)


def _pre_swa_prompt_kernel(x_ref, g_ref, wqt_ref, wk_ref, wkt_ref, wvt_ref, wz_ref,
                           cos_ref, sa_ref, sb_ref, cost_ref, sint_ref,
                           qta_ref, ka_ref, vta_ref, kt_ref, vt_ref, gate_ref):
    h = _rms_norm(x_ref[...], g_ref[...]).astype(BF16)
    cos_t, sin_t = cost_ref[...], sint_ref[...]
    _store_q_aug(qta_ref, _rope_fmaj(_dot_nt(wqt_ref[...], h), cos_t, sin_t, HEAD_DIM), None)
    ka_ref[...] = _rope_rows(_dot(h, wk_ref[...]), cos_ref[...], sa_ref[...], sb_ref[...]).astype(BF16)
    kt_ref[0] = _rope_fmaj(_dot_nt(wkt_ref[...], h), cos_t, sin_t, HEAD_DIM)
    vt = _dot_nt(wvt_ref[...], h)
    vt_ref[0] = vt
    _store_v_aug(vta_ref, vt)
    gate_ref[...] = _silu(_dot(h, wz_ref[...])).astype(BF16)


def _pre_swa_prompt(x, g, w, tables, nb):
    rows = x.shape[0]
    seq = rows // nb
    tm = min(256, seq)
    per = seq // tm
    row, fmaj = _row_specs(rows, nb, tm)
    tab = pl.BlockSpec((tm, LANES), lambda i: (i % per, 0))
    tab_t = pl.BlockSpec((ROT_HALF, tm), lambda i: (0, i % per))
    names = ("wqt", "wk", "wkt", "wvt", "wz")
    return pl.pallas_call(
        _pre_swa_prompt_kernel,
        grid=(rows // tm,),
        in_specs=[row(D_MODEL), _full((1, D_MODEL))] + [_full(w[n].shape) for n in names]
                 + [tab, tab, tab, tab_t, tab_t],
        out_specs=[fmaj(Q_AUG_ROWS), row(K_AUG_LANES), fmaj(N_KV * V_AUG), fmaj(KV_WIDTH), fmaj(KV_WIDTH),
                   row(D_MODEL)],
        out_shape=[jax.ShapeDtypeStruct((nb, Q_AUG_ROWS, seq), BF16), jax.ShapeDtypeStruct((rows, K_AUG_LANES), BF16),
                   jax.ShapeDtypeStruct((nb, N_KV * V_AUG, seq), BF16),
                   jax.ShapeDtypeStruct((nb, KV_WIDTH, seq), F32), jax.ShapeDtypeStruct((nb, KV_WIDTH, seq), F32),
                   jax.ShapeDtypeStruct((rows, D_MODEL), BF16)],
        compiler_params=_cparams("parallel"),
        name="pre_swa_prompt",
    )(x, g, *[w[n] for n in names], *tables)


def _pre_fox_decode_kernel(x_ref, g_ref, wq_ref, wkt_ref, wvt_ref, wft_ref, wz_ref, bft_ref,
                           q_ref, kt_ref, vt_ref, lft_ref, gate_ref):
    h = _rms_norm(x_ref[...], g_ref[...]).astype(BF16)
    q_ref[...] = _dot(h, wq_ref[...]).astype(BF16)
    kt_ref[0] = _dot_nt(wkt_ref[...], h)
    vt_ref[0] = _dot_nt(wvt_ref[...], h)
    lft_ref[0] = _log_sigmoid(_dot_nt(wft_ref[...], h) + bft_ref[...])
    gate_ref[...] = _silu(_dot(h, wz_ref[...])).astype(BF16)


def _pre_fox_decode(x, g, w):
    rows = x.shape[0]
    row, fmaj = _row_specs(rows, 1, rows)
    names = ("wq", "wkt", "wvt", "wft", "wz", "bft")
    return pl.pallas_call(
        _pre_fox_decode_kernel,
        grid=(1,),
        in_specs=[row(D_MODEL), _full((1, D_MODEL))] + [_full(w[n].shape) for n in names],
        out_specs=[row(D_MODEL), fmaj(KV_WIDTH), fmaj(KV_WIDTH), fmaj(N_HEADS), row(D_MODEL)],
        out_shape=[jax.ShapeDtypeStruct((rows, D_MODEL), BF16),
                   jax.ShapeDtypeStruct((1, KV_WIDTH, rows), F32), jax.ShapeDtypeStruct((1, KV_WIDTH, rows), F32),
                   jax.ShapeDtypeStruct((1, N_HEADS, rows), F32), jax.ShapeDtypeStruct((rows, D_MODEL), BF16)],
        compiler_params=_cparams("arbitrary"),
        name="pre_fox_decode",
    )(x, g, *[w[n] for n in names])


def _pre_swa_decode_kernel(x_ref, g_ref, wq_ref, wkt_ref, wvt_ref, wz_ref, cos_ref, sa_ref, sb_ref,
                           cost_ref, sint_ref, q_ref, kt_ref, vt_ref, gate_ref):
    h = _rms_norm(x_ref[...], g_ref[...]).astype(BF16)
    q_ref[...] = _rope_rows(_dot(h, wq_ref[...]), cos_ref[...], sa_ref[...], sb_ref[...]).astype(BF16)
    kt_ref[0] = _rope_fmaj(_dot_nt(wkt_ref[...], h), cost_ref[...], sint_ref[...], HEAD_DIM)
    vt_ref[0] = _dot_nt(wvt_ref[...], h)
    gate_ref[...] = _silu(_dot(h, wz_ref[...])).astype(BF16)


def _pre_swa_decode(x, g, w, tables):
    rows = x.shape[0]
    row, fmaj = _row_specs(rows, 1, rows)
    names = ("wq", "wkt", "wvt", "wz")
    return pl.pallas_call(
        _pre_swa_decode_kernel,
        grid=(1,),
        in_specs=[row(D_MODEL), _full((1, D_MODEL))] + [_full(w[n].shape) for n in names]
                 + [_full(t.shape) for t in tables],
        out_specs=[row(D_MODEL), fmaj(KV_WIDTH), fmaj(KV_WIDTH), row(D_MODEL)],
        out_shape=[jax.ShapeDtypeStruct((rows, D_MODEL), BF16),
                   jax.ShapeDtypeStruct((1, KV_WIDTH, rows), F32), jax.ShapeDtypeStruct((1, KV_WIDTH, rows), F32),
                   jax.ShapeDtypeStruct((rows, D_MODEL), BF16)],
        compiler_params=_cparams("arbitrary"),
        name="pre_swa_decode",
    )(x, g, *[w[n] for n in names], *tables)


def _pre_conv_kernel(x_ref, g_ref, wa_ref, wg_ref, wz_ref, glu_ref, gate_ref):
    h = _rms_norm(x_ref[...], g_ref[...]).astype(BF16)
    glu_ref[...] = _dot(h, wa_ref[...]) * _sigmoid(_dot(h, wg_ref[...]))
    gate_ref[...] = _silu(_dot(h, wz_ref[...])).astype(BF16)


def _pre_conv(x, g, w):
    rows = x.shape[0]
    tm = min(512, rows)
    row = pl.BlockSpec((tm, D_MODEL), lambda i: (i, 0))
    return pl.pallas_call(
        _pre_conv_kernel,
        grid=(rows // tm,),
        in_specs=[row, _full((1, D_MODEL)), _full(w["wa"].shape), _full(w["wg"].shape), _full(w["wz"].shape)],
        out_specs=[row, row],
        out_shape=[jax.ShapeDtypeStruct((rows, D_MODEL), F32), jax.ShapeDtypeStruct((rows, D_MODEL), BF16)],
        compiler_params=_cparams("parallel"),
        name="pre_conv",
    )(x, g, w["wa"], w["wg"], w["wz"])


def _post_mix_kernel(*refs, gated, final):
    refs = list(refs)
    a_ref = refs.pop(0)
    gate_ref = refs.pop(0) if gated else None
    x_ref, p_ref, wo_ref, ng_ref, wg_ref, bg_ref, wp_ref = refs[:7]
    fg_ref = refs[7] if final else None
    o_ref = refs[-1]
    a = a_ref[...]
    if gated:
        a = (a * gate_ref[...].astype(F32)).astype(BF16)
    x1 = x_ref[...] + _dot(a, wo_ref[...])
    h = _rms_norm(x1, ng_ref[...]).astype(BF16)
    gate2 = _sigmoid(_dot(h, wg_ref[...]) + bg_ref[...])
    x2 = x1 + gate2 * _dot(p_ref[...].astype(BF16), wp_ref[...])
    if final:
        x2 = _rms_norm(x2, fg_ref[...])
    o_ref[...] = x2


def _post_mix(a, gate, x, p_all, layer, w, final_g):
    rows = x.shape[0]
    tm = min(512, rows)
    row = lambda width: pl.BlockSpec((tm, width), lambda i: (i, 0))
    gated, final = gate is not None, final_g is not None
    pe = p_all.shape[2]
    args = [a] + ([gate] if gated else []) + [x, p_all, w["wo"], w["ng"], w["wg"], w["bg"], w["wp"]]
    specs = [row(D_MODEL)] * (2 if gated else 1)
    specs += [row(D_MODEL), pl.BlockSpec((None, tm, pe), lambda i: (layer, i, 0)),
              _full((D_MODEL, D_MODEL)), _full((1, D_MODEL)), _full((D_MODEL, D_MODEL)),
              _full((1, D_MODEL)), _full((pe, D_MODEL))]
    if final:
        args.append(final_g)
        specs.append(_full((1, D_MODEL)))
    return pl.pallas_call(
        functools.partial(_post_mix_kernel, gated=gated, final=final),
        grid=(rows // tm,),
        in_specs=specs,
        out_specs=row(D_MODEL),
        out_shape=jax.ShapeDtypeStruct((rows, D_MODEL), F32),
        compiler_params=_cparams("parallel"),
        name="post_mix",
    )(*args)


def _softmax_step(st, m, acc, vta):
    m_new = jnp.maximum(m, jnp.max(st, axis=0, keepdims=True))
    alpha = jnp.exp2(m - m_new)
    p = jnp.exp2(st - m_new).astype(BF16)
    return m_new, alpha * acc + _dot(vta, p)


def _finish_heads(accs, denoms, gate_ref, o_ref):
    outs = [(acc[0:HEAD_DIM, :] / den).T for acc, den in zip(accs, denoms)]
    o_ref[...] = (jnp.concatenate(outs, axis=1) * gate_ref[...].astype(F32)).astype(BF16)


def _fox_prompt_kernel(qta_ref, ka_ref, vta_ref, gate_ref, o_ref, *, tq):
    i = pl.program_id(2)
    half = tq // 2
    qts = [qta_ref[0, g * QK_AUG:(g + 1) * QK_AUG, :] for g in range(GROUP)]

    def step(j, carry):
        off = pl.multiple_of(j * tq, tq)
        ka = ka_ref[pl.ds(off, tq), :]
        vta = vta_ref[0, :, pl.ds(off, tq)]
        sts = [_dot(ka, qts[g]) for g in range(GROUP)]
        return tuple(_softmax_step(sts[g], carry[g][0], carry[g][1], vta) for g in range(GROUP))

    init = tuple((jnp.full((1, tq), -jnp.inf, F32), jnp.zeros((V_AUG, tq), F32)) for _ in range(GROUP))
    carry = lax.fori_loop(0, i, step, init)

    off = pl.multiple_of(i * tq, tq)
    ka = ka_ref[pl.ds(off, tq), :]
    vta = vta_ref[0, :, pl.ds(off, tq)]
    tri_a = (lax.broadcasted_iota(jnp.int32, (half, half), 0) <= lax.broadcasted_iota(jnp.int32, (half, half), 1))
    tri_b = (lax.broadcasted_iota(jnp.int32, (tq, half), 0)
             <= lax.broadcasted_iota(jnp.int32, (tq, half), 1) + half)
    sts_a = [jnp.where(tri_a, _dot(ka[0:half, :], qts[g][:, 0:half]), -jnp.inf) for g in range(GROUP)]
    sts_b = [jnp.where(tri_b, _dot(ka, qts[g][:, half:]), -jnp.inf) for g in range(GROUP)]
    accs = []
    for g in range(GROUP):
        m, acc = carry[g]
        _, acc_a = _softmax_step(sts_a[g], m[:, 0:half], acc[:, 0:half], vta[:, 0:half])
        _, acc_b = _softmax_step(sts_b[g], m[:, half:], acc[:, half:], vta)
        accs.append(jnp.concatenate([acc_a, acc_b], axis=1))
    _finish_heads(accs, [acc[HEAD_DIM:HEAD_DIM + 1, :] for acc in accs], gate_ref, o_ref)


def _fox_prompt(qta, ka, vta, gate):
    batch, _, seq = qta.shape
    rows = batch * seq
    tq = min(512, seq)
    assert seq % tq == 0 and tq % (2 * LANES) == 0
    nq = seq // tq
    width = GROUP * HEAD_DIM
    qspec = pl.BlockSpec((tq, width), lambda b, h, i: (b * nq + i, h))
    return pl.pallas_call(
        functools.partial(_fox_prompt_kernel, tq=tq),
        grid=(batch, N_KV, nq),
        in_specs=[pl.BlockSpec((1, GROUP * QK_AUG, tq), lambda b, h, i: (b, h, i)),
                  pl.BlockSpec((seq, QK_AUG), lambda b, h, i: (b, h)),
                  pl.BlockSpec((1, V_AUG, seq), lambda b, h, i: (b, h, 0)),
                  qspec],
        out_specs=qspec,
        out_shape=jax.ShapeDtypeStruct((rows, D_MODEL), BF16),
        compiler_params=_cparams("parallel", "parallel", "arbitrary"),
        name="fox_prompt",
    )(qta, ka, vta, gate)


SWA_QB = 4


def _swa_prompt_kernel(qta_ref, ka_ref, vta_ref, sink_ref, gate_ref, o_ref):
    n0 = pl.program_id(2) * SWA_QB
    ki = lax.broadcasted_iota(jnp.int32, (2 * Q_BLOCK, Q_BLOCK), 0)
    qi = lax.broadcasted_iota(jnp.int32, (2 * Q_BLOCK, Q_BLOCK), 1)
    starts, sts = [], []
    for r in range(SWA_QB):
        n = n0 + r
        start = pl.multiple_of(jnp.maximum(n - 1, 0) * Q_BLOCK, Q_BLOCK)
        rel = (n * Q_BLOCK + qi) - (start + ki)
        allowed = (rel >= 0) & (rel < WINDOW)
        ka = ka_ref[pl.ds(start, 2 * Q_BLOCK), :]
        starts.append(start)
        sts.append([jnp.where(allowed, _dot(ka, qta_ref[0, g * QK_AUG:(g + 1) * QK_AUG,
                                                        r * Q_BLOCK:(r + 1) * Q_BLOCK]), -jnp.inf)
                    for g in range(GROUP)])
    sinks = [sink_ref[0, :, g:g + 1] * LOG2E for g in range(GROUP)]
    ms = [[jnp.maximum(jnp.max(sts[r][g], axis=0, keepdims=True), sinks[g]) for g in range(GROUP)]
          for r in range(SWA_QB)]
    ps = [[jnp.exp2(sts[r][g] - ms[r][g]).astype(BF16) for g in range(GROUP)] for r in range(SWA_QB)]
    for r in range(SWA_QB):
        vta = vta_ref[0, :, pl.ds(starts[r], 2 * Q_BLOCK)]
        accs = [_dot(vta, ps[r][g]) for g in range(GROUP)]
        denoms = [accs[g][HEAD_DIM:HEAD_DIM + 1, :] + jnp.exp2(sinks[g] - ms[r][g]) for g in range(GROUP)]
        rows = slice(r * Q_BLOCK, (r + 1) * Q_BLOCK)
        _finish_heads(accs, denoms, gate_ref.at[rows, :], o_ref.at[rows, :])


def _swa_prompt(qta, ka, vta, sinks, gate):
    batch, _, seq = qta.shape
    rows = batch * seq
    tq = SWA_QB * Q_BLOCK
    nq = seq // tq
    width = GROUP * HEAD_DIM
    qspec = pl.BlockSpec((tq, width), lambda b, h, i: (b * nq + i, h))
    return pl.pallas_call(
        _swa_prompt_kernel,
        grid=(batch, N_KV, nq),
        in_specs=[pl.BlockSpec((1, GROUP * QK_AUG, tq), lambda b, h, i: (b, h, i)),
                  pl.BlockSpec((seq, QK_AUG), lambda b, h, i: (b, h)),
                  pl.BlockSpec((1, V_AUG, seq), lambda b, h, i: (b, h, 0)),
                  pl.BlockSpec((1, 1, GROUP), lambda b, h, i: (h, 0, 0)),
                  qspec],
        out_specs=qspec,
        out_shape=jax.ShapeDtypeStruct((rows, D_MODEL), BF16),
        compiler_params=_cparams("parallel", "parallel", "arbitrary"),
        name="swa_prompt",
    )(qta, ka, vta, sinks.reshape(N_KV, 1, GROUP), gate)


def _block_diag_mask():
    r = lax.broadcasted_iota(jnp.int32, (DEC_ROWS, KV_WIDTH), 0)
    c = lax.broadcasted_iota(jnp.int32, (DEC_ROWS, KV_WIDTH), 1)
    return ((r % N_HEADS) // GROUP) == (c // HEAD_DIM)


def _fold_heads(o_full):
    om = jnp.where(_block_diag_mask(), o_full, 0.0)
    return (om[:, 0:HEAD_DIM] + om[:, HEAD_DIM:2 * HEAD_DIM]
            + om[:, 2 * HEAD_DIM:3 * HEAD_DIM] + om[:, 3 * HEAD_DIM:4 * HEAD_DIM])


def _new_token_mask():
    lane = lax.broadcasted_iota(jnp.int32, (DEC_ROWS, PAGE), 1)
    t = lax.broadcasted_iota(jnp.int32, (DEC_ROWS, PAGE), 0) // N_HEADS
    return (lane >= NEW_LANE) & (lane - NEW_LANE <= t)


def _fox_decode_kernel(pt_ref, qbd_ref, knew_ref, vnew_ref, lfnew_ref, kc_ref, vc_ref, lc_ref, o_ref,
                       kbuf, vbuf, lbuf, sems, *, layer, cp):
    b = pl.program_id(0)
    nb = pl.num_programs(0)
    nch = pt_ref.shape[1] // cp

    def copies(seq, ci, slot):
        base = (nch - 1 - ci) * cp
        out = []
        for p in range(cp):
            page = pt_ref[seq, base + p]
            out.append(pltpu.make_async_copy(kc_ref.at[layer, page], kbuf.at[slot, p], sems.at[0, slot]))
            out.append(pltpu.make_async_copy(vc_ref.at[layer, page], vbuf.at[slot, p], sems.at[1, slot]))
            out.append(pltpu.make_async_copy(lc_ref.at[layer, page], lbuf.at[slot, p], sems.at[2, slot]))
        return out

    total = nb * nch

    def start_chunk(g):
        for c in copies(g // nch, g % nch, g % DEC_SLOTS):
            c.start()

    @pl.when(b == 0)
    def _():
        for g in range(DEC_SLOTS - 1):
            start_chunk(g)

    qbd = qbd_ref[0]

    x = lfnew_ref[0]
    cn = x + pltpu.roll(x, 1, 1) + pltpu.roll(x, 2, 1) + pltpu.roll(x, 3, 1)
    s_new = _dot(qbd, knew_ref[0].astype(BF16))
    cqs = [cn[:, NEW_LANE + t:NEW_LANE + t + 1] for t in range(DEC_T)]
    s_new = s_new + jnp.concatenate([cqs[t] - cn for t in range(DEC_T)], axis=0)
    s_new = jnp.where(_new_token_mask(), s_new, -jnp.inf)
    cq = jnp.concatenate(cqs, axis=0)
    m = jnp.max(s_new, axis=-1, keepdims=True)
    p = jnp.exp(s_new - m)
    l = jnp.sum(p, axis=-1, keepdims=True)
    acc = _dot_nt(p.astype(BF16), vnew_ref[0].astype(BF16))

    sfx = jnp.concatenate([_tri(PAGE, strict=True, upper=False), jnp.ones((PAGE, LANES), BF16)], axis=1)

    def chunk(ci, carry):
        m, l, acc, later = carry
        g = b * nch + ci
        slot = g % DEC_SLOTS
        for c in copies(b, ci, slot):
            c.wait()

        @pl.when(g + DEC_SLOTS - 1 < total)
        def _():
            start_chunk(g + DEC_SLOTS - 1)

        st = _dot3(_split3(lbuf[slot].reshape(cp * N_HEADS, PAGE)), sfx).reshape(cp, N_HEADS, 2 * LANES)
        bias = [None] * cp
        for pg in reversed(range(cp)):
            bias[pg] = jnp.concatenate([st[pg, :, :LANES] + later] * DEC_T, axis=0) + cq
            later = later + st[pg, :, LANES:]
        ss = [_dot(qbd, kbuf[slot, pg].astype(BF16)) + bias[pg] for pg in range(cp)]
        per = cp // DEC_GROUPS if cp % DEC_GROUPS == 0 else cp
        parts = []
        for lo in range(0, cp, per):
            grp = ss[lo:lo + per]
            m_loc = jnp.max(functools.reduce(jnp.maximum, grp), axis=-1, keepdims=True)
            ps = [jnp.exp(s - m_loc) for s in grp]
            l_loc = jnp.sum(functools.reduce(jnp.add, ps), axis=-1, keepdims=True)
            parts.append((m_loc, l_loc, lo, ps))
        accs = []
        for m_loc, l_loc, lo, ps in parts:
            a = _dot_nt(ps[0].astype(BF16), vbuf[slot, lo].astype(BF16))
            for n in range(1, len(ps)):
                a = a + _dot_nt(ps[n].astype(BF16), vbuf[slot, lo + n].astype(BF16))
            accs.append(a)
        m_new = functools.reduce(jnp.maximum, [m] + [pt[0] for pt in parts])
        alpha = jnp.exp(m - m_new)
        l, acc = alpha * l, alpha * acc
        for (m_loc, l_loc, _, _), a in zip(parts, accs):
            w = jnp.exp(m_loc - m_new)
            l, acc = l + w * l_loc, acc + w * a
        return m_new, l, acc, later

    m, l, acc, _ = lax.fori_loop(0, nch, chunk, (m, l, acc, jnp.zeros((N_HEADS, LANES), F32)))
    o_ref[0] = _fold_heads(acc / l)


def _fox_decode(page_table, qbd, knew, vnew, lfnew, cache_kt, cache_vt, cache_lft, layer):
    batch, n_pages = page_table.shape
    cp = 16
    while n_pages % cp:
        cp //= 2
    assert batch * (n_pages // cp) >= DEC_SLOTS - 1
    seq3 = lambda r, c: pl.BlockSpec((1, r, c), lambda b, pt: (b, 0, 0))
    grid_spec = pltpu.PrefetchScalarGridSpec(
        num_scalar_prefetch=1,
        grid=(batch,),
        in_specs=[seq3(DEC_ROWS, KV_WIDTH), seq3(KV_WIDTH, PAGE), seq3(KV_WIDTH, PAGE), seq3(N_HEADS, PAGE),
                  pl.BlockSpec(memory_space=pl.ANY), pl.BlockSpec(memory_space=pl.ANY),
                  pl.BlockSpec(memory_space=pl.ANY)],
        out_specs=seq3(DEC_ROWS, HEAD_DIM),
        scratch_shapes=[pltpu.VMEM((DEC_SLOTS, cp, KV_WIDTH, PAGE), F32),
                        pltpu.VMEM((DEC_SLOTS, cp, KV_WIDTH, PAGE), F32),
                        pltpu.VMEM((DEC_SLOTS, cp, N_HEADS, PAGE), F32), pltpu.SemaphoreType.DMA((3, DEC_SLOTS))],
    )
    return pl.pallas_call(
        functools.partial(_fox_decode_kernel, layer=layer, cp=cp),
        grid_spec=grid_spec,
        out_shape=jax.ShapeDtypeStruct((batch, DEC_ROWS, HEAD_DIM), F32),
        compiler_params=_cparams("arbitrary"),
        name="fox_decode",
    )(page_table, qbd, knew, vnew, lfnew, cache_kt, cache_vt, cache_lft)


def _swa_decode_kernel(qbd_ref, knew_ref, vnew_ref, sk_ref, sv_ref, sink_ref, o_ref, nk_ref, nv_ref):
    lane = lax.broadcasted_iota(jnp.int32, (DEC_ROWS, PAGE), 1)
    t = lax.broadcasted_iota(jnp.int32, (DEC_ROWS, PAGE), 0) // N_HEADS
    keep_new = lax.broadcasted_iota(jnp.int32, (KV_WIDTH, PAGE), 1) >= NEW_LANE
    sink = sink_ref[...]
    for s in range(qbd_ref.shape[0]):
        qbd = qbd_ref[s]
        sk, sv, kn, vn = sk_ref[0, s], sv_ref[0, s], knew_ref[s], vnew_ref[s]
        s_old = jnp.where(lane > t, _dot(qbd, sk.astype(BF16)), -jnp.inf)
        s_new = jnp.where(_new_token_mask(), _dot(qbd, kn.astype(BF16)), -jnp.inf)
        m = jnp.maximum(jnp.maximum(jnp.max(s_old, axis=-1, keepdims=True),
                                    jnp.max(s_new, axis=-1, keepdims=True)), sink)
        p_old = jnp.exp(s_old - m)
        p_new = jnp.exp(s_new - m)
        denom = (jnp.sum(p_old, axis=-1, keepdims=True) + jnp.sum(p_new, axis=-1, keepdims=True)
                 + jnp.exp(sink - m))
        o_full = _dot_nt(p_old.astype(BF16), sv.astype(BF16)) + _dot_nt(p_new.astype(BF16), vn.astype(BF16))
        o_ref[s] = _fold_heads(o_full / denom)
        nk_ref[s] = jnp.where(keep_new, kn, pltpu.roll(sk, NEW_LANE, 1))
        nv_ref[s] = jnp.where(keep_new, vn, pltpu.roll(sv, NEW_LANE, 1))


def _swa_decode(qbd, knew, vnew, state_kt, state_vt, sink_rows, layer):
    batch, buf = state_kt.shape[1], state_kt.shape[3]
    assert buf == WINDOW == PAGE
    sb = 4
    while batch % sb:
        sb //= 2
    seq3 = lambda r, c: pl.BlockSpec((sb, r, c), lambda b: (b, 0, 0))
    st = pl.BlockSpec((1, sb, KV_WIDTH, buf), lambda b: (layer, b, 0, 0))
    return pl.pallas_call(
        _swa_decode_kernel,
        grid=(batch // sb,),
        in_specs=[seq3(DEC_ROWS, KV_WIDTH), seq3(KV_WIDTH, PAGE), seq3(KV_WIDTH, PAGE), st, st,
                  _full((DEC_ROWS, 1))],
        out_specs=[seq3(DEC_ROWS, HEAD_DIM), seq3(KV_WIDTH, buf), seq3(KV_WIDTH, buf)],
        out_shape=[jax.ShapeDtypeStruct((batch, DEC_ROWS, HEAD_DIM), F32),
                   jax.ShapeDtypeStruct((batch, KV_WIDTH, buf), F32),
                   jax.ShapeDtypeStruct((batch, KV_WIDTH, buf), F32)],
        compiler_params=_cparams("parallel"),
        name="swa_decode",
    )(qbd, knew, vnew, state_kt, state_vt, sink_rows)


HALO = 32
CONV_CHUNK = 32


def _conv_tail(y, gate, lng, lnb):
    yc = y - jnp.mean(y, axis=-1, keepdims=True)
    yn = yc * lax.rsqrt(jnp.mean(yc * yc, axis=-1, keepdims=True) + NORM_EPS) * lng + lnb
    return (_silu(yn) * gate.astype(F32)).astype(BF16)


def _conv_prompt_kernel(x_ref, g_ref, wa_ref, wg_ref, wz_ref, dw_ref, db_ref, lng_ref, lnb_ref,
                        a_ref, st_ref, buf, shifted, gbuf):
    i = pl.program_id(1)
    tm = x_ref.shape[0]
    h = _rms_norm(x_ref[...], g_ref[...]).astype(BF16)
    glu = _dot(h, wa_ref[...]) * _sigmoid(_dot(h, wg_ref[...]))
    gate = _silu(_dot(h, wz_ref[...]))

    @pl.when(i == 0)
    def _():
        buf[0:HALO, :] = jnp.zeros((HALO, D_MODEL), F32)

    buf[HALO:HALO + tm, :] = glu
    gbuf[...] = gate
    first = HALO - CONV_HIST
    for r in range(SUBLANES):
        span = tm + SUBLANES * ((CONV_HIST - r) // SUBLANES)
        shifted[r, 0:span, :] = buf[first + r:first + r + span, :]

    def chunk(c, _):
        off = pl.multiple_of(c * CONV_CHUNK, CONV_CHUNK)
        y = jnp.zeros((CONV_CHUNK, D_MODEL), F32) + db_ref[...]
        for w in range(CONV_WIDTH):
            start = off + SUBLANES * (w // SUBLANES)
            y = y + shifted[w % SUBLANES, pl.ds(start, CONV_CHUNK), :] * dw_ref[w:w + 1, :]
        rows = pl.ds(off, CONV_CHUNK)
        a_ref[rows, :] = _conv_tail(y, gbuf[rows, :], lng_ref[...], lnb_ref[...])
        return 0

    lax.fori_loop(0, tm // CONV_CHUNK, chunk, 0, unroll=4)

    @pl.when(i == pl.num_programs(1) - 1)
    def _():
        st_ref[0] = buf[HALO + tm - CONV_HIST:HALO + tm, :]

    buf[0:HALO, :] = buf[tm:tm + HALO, :]


def _conv_prompt(x, g, w, batch, seq):
    rows = batch * seq
    tm = min(256, seq)
    nt = seq // tm
    row = pl.BlockSpec((tm, D_MODEL), lambda b, i: (b * nt + i, 0))
    return pl.pallas_call(
        _conv_prompt_kernel,
        grid=(batch, nt),
        in_specs=[row, _full((1, D_MODEL)), _full(w["wa"].shape), _full(w["wg"].shape), _full(w["wz"].shape),
                  _full((CONV_WIDTH, D_MODEL)), _full((1, D_MODEL)), _full((1, D_MODEL)), _full((1, D_MODEL))],
        out_specs=[row, pl.BlockSpec((1, CONV_HIST, D_MODEL), lambda b, i: (b, 0, 0))],
        out_shape=[jax.ShapeDtypeStruct((rows, D_MODEL), BF16),
                   jax.ShapeDtypeStruct((batch, CONV_HIST, D_MODEL), F32)],
        scratch_shapes=[pltpu.VMEM((HALO + tm, D_MODEL), F32),
                        pltpu.VMEM((SUBLANES, tm + SUBLANES * (CONV_HIST // SUBLANES), D_MODEL), F32),
                        pltpu.VMEM((tm, D_MODEL), F32)],
        compiler_params=_cparams("arbitrary", "arbitrary"),
        name="conv_prompt",
    )(x, g, w["wa"], w["wg"], w["wz"], w["dw"], w["db"], w["lng"], w["lnb"])


def _conv_decode_kernel(glu_ref, gate_ref, st_ref, dw_ref, db_ref, lng_ref, lnb_ref, a_ref, ns_ref, pad):
    sb = glu_ref.shape[1]
    pad[0:CONV_HIST] = st_ref[0]
    pad[CONV_HIST:CONV_HIST + DEC_T] = glu_ref[...]
    y = jnp.zeros((DEC_T, sb, D_MODEL), F32) + db_ref[...]
    for w in range(CONV_WIDTH):
        y = y + pad[w:w + DEC_T] * dw_ref[w:w + 1, :]
    a_ref[...] = _conv_tail(y, gate_ref[...], lng_ref[...], lnb_ref[...])
    ns_ref[...] = pad[DEC_T:DEC_T + CONV_HIST]


def _conv_decode(glu, gate, state, w, layer):
    batch = state.shape[2]
    sb = 16
    while batch % sb:
        sb //= 2
    tok = pl.BlockSpec((DEC_T, sb, D_MODEL), lambda i: (0, i, 0))
    return pl.pallas_call(
        _conv_decode_kernel,
        grid=(batch // sb,),
        in_specs=[tok, tok, pl.BlockSpec((1, CONV_HIST, sb, D_MODEL), lambda i: (layer, 0, i, 0)),
                  _full((CONV_WIDTH, D_MODEL)), _full((1, D_MODEL)), _full((1, D_MODEL)), _full((1, D_MODEL))],
        out_specs=[tok, pl.BlockSpec((CONV_HIST, sb, D_MODEL), lambda i: (0, i, 0))],
        out_shape=[jax.ShapeDtypeStruct((DEC_T, batch, D_MODEL), BF16),
                   jax.ShapeDtypeStruct((CONV_HIST, batch, D_MODEL), F32)],
        scratch_shapes=[pltpu.VMEM((CONV_HIST + DEC_T, sb, D_MODEL), F32)],
        compiler_params=_cparams("parallel"),
        name="conv_decode",
    )(glu.reshape(DEC_T, batch, D_MODEL), gate.reshape(DEC_T, batch, D_MODEL), state,
      w["dw"], w["db"], w["lng"], w["lnb"])


def _rope_tables(pos):
    inv_freq = ROPE_THETA ** (-jnp.arange(ROT_HALF, dtype=F32) / ROT_HALF)
    ang = pos.astype(F32)[:, None] * inv_freq[None, :]
    cos, sin = jnp.cos(ang), jnp.sin(ang)
    n = pos.shape[0]
    rest = jnp.zeros((n, HEAD_DIM - ROT_DIM), F32)
    zero = jnp.zeros((n, ROT_HALF), F32)
    cos_h = jnp.concatenate([cos, cos, rest + 1.0], axis=1)
    sa_h = jnp.concatenate([-sin, zero, rest], axis=1)
    sb_h = jnp.concatenate([zero, sin, rest], axis=1)
    rep = LANES // HEAD_DIM
    return tuple(jnp.tile(t, (1, rep)) for t in (cos_h, sa_h, sb_h)) + (cos.T, sin.T)


def _decode_queries(q_rows, batch):
    q4 = q_rows.reshape(DEC_T, batch, N_HEADS, 1, HEAD_DIM).transpose(1, 0, 2, 3, 4)
    own = (jnp.arange(N_HEADS)[:, None] // GROUP) == jnp.arange(N_KV)[None, :]
    return jnp.where(own[None, None, :, :, None], q4, jnp.zeros((), q_rows.dtype)).reshape(batch, DEC_ROWS, KV_WIDTH)


def _new_token_pages(xt, batch):
    width = xt.shape[1]
    per_seq = xt.reshape(width, DEC_T, batch).transpose(2, 0, 1)
    return jnp.pad(per_seq, ((0, 0), (0, 0), (NEW_LANE, 0)))


def _decode_out_rows(o, batch):
    return o.reshape(batch, DEC_T, D_MODEL).transpose(1, 0, 2).reshape(DEC_T * batch, D_MODEL)


def _sample_major(xt, batch, inner):
    return xt.reshape(inner + (DEC_T, batch)).transpose((len(inner) + 1, len(inner)) + tuple(range(len(inner))))


def kernel(x_prompt, x_sample, cache_fox_k, cache_fox_v, cache_fox_lf, page_table, state_swa_k, state_swa_v,
           state_conv, p_prompt, p_sample, norm_g, fox_w_in, fox_b_f, fox_w_out, swa_w_in, swa_sinks, swa_w_out,
           conv_w_in, conv_dw_w, conv_dw_b, conv_ln_g, conv_ln_b, conv_w_out, pe_w_proj, pe_norm_g, pe_w_gate,
           pe_b_gate, final_norm_g):
    batch, seq, _ = x_prompt.shape
    dec_batch, dec_t, _ = x_sample.shape
    assert dec_t == DEC_T and seq % (SWA_QB * Q_BLOCK) == 0
    depth = norm_g.shape[0]
    past_len = page_table.shape[1] * PAGE
    buf = state_swa_k.shape[2]
    rows_p, rows_s = batch * seq, dec_batch * dec_t
    att = N_HEADS * HEAD_DIM

    xp = x_prompt.reshape(rows_p, D_MODEL)
    xs = x_sample.transpose(1, 0, 2).reshape(rows_s, D_MODEL)
    pp_all = p_prompt.reshape(depth, rows_p, -1)
    ps_all = p_sample.transpose(0, 2, 1, 3).reshape(depth, rows_s, -1)
    fmaj = lambda c: c.transpose(0, 1, 3, 4, 2).reshape(c.shape[:2] + (KV_WIDTH, c.shape[2]))
    cache_kt, cache_vt = fmaj(cache_fox_k), fmaj(cache_fox_v)
    cache_lft = cache_fox_lf.transpose(0, 1, 3, 2)
    st_kt, st_vt = fmaj(state_swa_k), fmaj(state_swa_v)
    st_conv = state_conv.transpose(0, 2, 1, 3)
    heads_major = lambda t: t.reshape(t.shape[0], N_KV, HEAD_DIM, t.shape[2]).transpose(0, 3, 1, 2)

    fox_k_p, fox_v_p, fox_lf_p, fox_k_s, fox_v_s, fox_lf_s = [], [], [], [], [], []
    swa_k_p, swa_v_p, swa_k_s, swa_v_s = [], [], [], []
    conv_p, conv_s = [], []

    for i in range(depth):
        kind, j = i % N_MIXERS, i // N_MIXERS
        g = norm_g[i].reshape(1, D_MODEL)
        if kind == 0:
            w_in = fox_w_in[j]
            wq, wk = w_in[:, :att], w_in[:, att:att + KV_WIDTH]
            wv = w_in[:, att + KV_WIDTH:att + 2 * KV_WIDTH]
            wf = w_in[:, att + 2 * KV_WIDTH:att + 2 * KV_WIDTH + N_HEADS]
            w = dict(_attn_weights(wq, wk, wv), wq=(wq * ATT_SCALE).astype(BF16),
                     wf=wf.astype(BF16), wft=wf.T.astype(BF16),
                     wz=w_in[:, att + 2 * KV_WIDTH + N_HEADS:].astype(BF16),
                     bf=fox_b_f[j].reshape(1, N_HEADS), bft=fox_b_f[j].reshape(N_HEADS, 1))
            w_out = fox_w_out[j]
            qta, ka, vta, kt, vt, lft, gate = _pre_fox_prompt(xp, g, w, batch)
            ap = _fox_prompt(qta, ka, vta, gate)
            gate_p = None
            fox_k_p.append(heads_major(kt))
            fox_v_p.append(heads_major(vt))
            fox_lf_p.append(lft.transpose(0, 2, 1))
            q, kt, vt, lft, gate_s = _pre_fox_decode(xs, g, w)
            o = _fox_decode(page_table, _decode_queries(q, dec_batch),
                            _new_token_pages(kt.astype(BF16), dec_batch), _new_token_pages(vt.astype(BF16), dec_batch),
                            _new_token_pages(lft, dec_batch), cache_kt, cache_vt, cache_lft, j)
            a_s = _decode_out_rows(o, dec_batch)
            fox_k_s.append(_sample_major(kt, dec_batch, (N_KV, HEAD_DIM)))
            fox_v_s.append(_sample_major(vt, dec_batch, (N_KV, HEAD_DIM)))
            fox_lf_s.append(_sample_major(lft, dec_batch, (N_HEADS,)))
        elif kind == 1:
            w_in = swa_w_in[j]
            wq, wk = w_in[:, :att], w_in[:, att:att + KV_WIDTH]
            wv = w_in[:, att + KV_WIDTH:att + 2 * KV_WIDTH]
            w = dict(_attn_weights(wq, wk, wv), wq=(wq * ATT_SCALE).astype(BF16),
                     wz=w_in[:, att + 2 * KV_WIDTH:].astype(BF16))
            w_out = swa_w_out[j]
            tables = _rope_tables(jnp.arange(seq, dtype=jnp.int32))
            qta, ka, vta, kt, vt, gate = _pre_swa_prompt(xp, g, w, tables, batch)
            ap = _swa_prompt(qta, ka, vta, swa_sinks[j], gate)
            gate_p = None
            swa_k_p.append(heads_major(kt[:, :, seq - buf:]))
            swa_v_p.append(heads_major(vt[:, :, seq - buf:]))
            pos_s = past_len + jnp.repeat(jnp.arange(DEC_T, dtype=jnp.int32), dec_batch)
            q, kt, vt, gate_s = _pre_swa_decode(xs, g, w, _rope_tables(pos_s))
            sink_rows = jnp.tile(swa_sinks[j].astype(F32), DEC_T).reshape(DEC_ROWS, 1)
            o, nkt, nvt = _swa_decode(_decode_queries(q, dec_batch), _new_token_pages(kt, dec_batch),
                                      _new_token_pages(vt, dec_batch), st_kt, st_vt, sink_rows, j)
            a_s = _decode_out_rows(o, dec_batch)
            swa_k_s.append(heads_major(nkt))
            swa_v_s.append(heads_major(nvt))
        else:
            w_in = conv_w_in[j]
            cd = w_in.shape[1] // 3
            w = dict(wa=w_in[:, :cd].astype(BF16), wg=w_in[:, cd:2 * cd].astype(BF16),
                     wz=w_in[:, 2 * cd:].astype(BF16), dw=conv_dw_w[j], db=conv_dw_b[j].reshape(1, cd),
                     lng=conv_ln_g[j].reshape(1, cd), lnb=conv_ln_b[j].reshape(1, cd))
            w_out = conv_w_out[j]
            ap, st = _conv_prompt(xp, g, w, batch, seq)
            gate_p = None
            conv_p.append(st)
            glu, gate = _pre_conv(xs, g, w)
            a_s, ns = _conv_decode(glu, gate, st_conv, w, j)
            a_s = a_s.reshape(rows_s, D_MODEL)
            gate_s = None
            conv_s.append(ns.transpose(1, 0, 2))
        wpost = dict(wo=w_out.astype(BF16), ng=pe_norm_g[i].reshape(1, D_MODEL), wg=pe_w_gate[i].astype(BF16),
                     bg=pe_b_gate[i].reshape(1, D_MODEL), wp=pe_w_proj[i].astype(BF16))
        fg = final_norm_g.reshape(1, D_MODEL) if i == depth - 1 else None
        xp = _post_mix(ap, gate_p, xp, pp_all, i, wpost, fg)
        xs = _post_mix(a_s, gate_s, xs, ps_all, i, wpost, fg)

    return (xp.reshape(batch, seq, D_MODEL), xs.reshape(DEC_T, dec_batch, D_MODEL).transpose(1, 0, 2),
            jnp.stack(fox_k_p), jnp.stack(fox_v_p), jnp.stack(fox_lf_p),
            jnp.stack(fox_k_s), jnp.stack(fox_v_s), jnp.stack(fox_lf_s),
            jnp.stack(swa_k_p), jnp.stack(swa_v_p), jnp.stack(swa_k_s), jnp.stack(swa_v_s),
            jnp.stack(conv_p), jnp.stack(conv_s))
```
